```python
import jax, jax.numpy as jnp
from jax import lax
import numpy as np

D_MODEL = 1024
BATCH = 4
SEQ = 8192
DEPTH = 1
DEC_BATCH = 8
DEC_SEQ = 2048
PAST_LEN = 128

MIX_WIDTH = D_MODEL
A_WIDTH = MIX_WIDTH // 2
B_WIDTH = MIX_WIDTH - A_WIDTH
POOL_WINDOWS = (2, 4, 8, 16)
POOL_GROUPS = len(POOL_WINDOWS)
POOL_CH = A_WIDTH // POOL_GROUPS
CHUNK = 128
SGU_HEADS = 4
SGU_HEAD_DIM = B_WIDTH // SGU_HEADS
IN_WIDTH = 2 * A_WIDTH + 3 * B_WIDTH
EPS = 1e-6

kernel_name = "hybrid_pool_sgu_encoder"


def rmsnorm(x, g):
    x32 = x.astype(jnp.float32)
    y = x32 * lax.rsqrt(jnp.mean(x32 * x32, axis=-1, keepdims=True) + EPS)
    return (y * g.astype(jnp.float32)).astype(x.dtype)


def layernorm(x, g, b):
    x32 = x.astype(jnp.float32)
    mu = jnp.mean(x32, axis=-1, keepdims=True)
    xc = x32 - mu
    var = jnp.mean(xc * xc, axis=-1, keepdims=True)
    y = xc * lax.rsqrt(var + EPS) * g.astype(jnp.float32) + b.astype(jnp.float32)
    return y.astype(x.dtype)


def pool_mixer(a, pool_w, pool_scale):
    bsz, s, _ = a.shape
    a32 = a.reshape(bsz, s, POOL_GROUPS, POOL_CH).astype(jnp.float32)
    csum = jnp.concatenate(
        [jnp.zeros((bsz, 1, POOL_GROUPS, POOL_CH), jnp.float32), jnp.cumsum(a32, axis=1)], axis=1)
    t = jnp.arange(s)
    pooled = []
    for g, w in enumerate(POOL_WINDOWS):
        lo = jnp.clip(t - w // 2, 0, s)
        hi = jnp.clip(t + w // 2, 0, s)
        win_sum = csum[:, hi, g] - csum[:, lo, g]
        cnt = (hi - lo).astype(jnp.float32)[None, :, None]
        pooled.append(win_sum / cnt)
    pooled = jnp.stack(pooled, axis=2)
    diff = (pooled - a32).astype(a.dtype)
    y = jnp.einsum('bsgc,gcd->bsgd', diff, pool_w)
    return y.reshape(bsz, s, A_WIDTH) * pool_scale


def spatial_gating(u, v, ln_g, ln_b, w_s, b_s):
    bsz, s, _ = u.shape
    vn = layernorm(v, ln_g, ln_b)
    vc = vn.reshape(bsz, s // CHUNK, CHUNK, SGU_HEADS, SGU_HEAD_DIM)
    mixed = jnp.einsum('hpq,bnqhd->bnphd', w_s, vc) + b_s.T[None, None, :, :, None]
    return u * mixed.reshape(bsz, s, B_WIDTH)


def trunk(x, norm_g, w_in, pool_w, pool_scale, sgu_ln_g, sgu_ln_b, w_spatial, b_spatial,
          w_out, final_g):
    for _ in range(DEPTH):
        h = rmsnorm(x, norm_g)
        p = jnp.einsum('bsd,de->bse', h, w_in)
        a = p[..., :A_WIDTH]
        gate_a = p[..., A_WIDTH:2 * A_WIDTH]
        o = 2 * A_WIDTH
        u = p[..., o:o + B_WIDTH]
        v = p[..., o + B_WIDTH:o + 2 * B_WIDTH]
        gate_b = p[..., o + 2 * B_WIDTH:]
        y_a = pool_mixer(a, pool_w, pool_scale) * jax.nn.silu(gate_a)
        y_b = spatial_gating(u, v, sgu_ln_g, sgu_ln_b, w_spatial, b_spatial) * jax.nn.silu(gate_b)
        mix = jnp.concatenate([y_a, y_b], axis=-1)
        x = x + jnp.einsum('bse,ed->bsd', mix, w_out)
    return rmsnorm(x, final_g)


def setup_inputs(seed: int = 0) -> dict:
    key = jax.random.key(seed)
    ks = jax.random.split(key, 12)
    f32 = jnp.float32
    return {
        "x_prompt": jax.random.normal(ks[0], (BATCH, SEQ, D_MODEL), f32),
        "x_sample": jax.random.normal(ks[1], (DEC_BATCH, DEC_SEQ, D_MODEL), f32),
        "norm_g": 1.0 + 0.1 * jax.random.normal(ks[2], (D_MODEL,), f32),
        "w_in": jax.random.normal(ks[3], (D_MODEL, IN_WIDTH), f32) * D_MODEL ** -0.5,
        "pool_w": jax.random.normal(ks[4], (POOL_GROUPS, POOL_CH, POOL_CH), f32) * POOL_CH ** -0.5,
        "pool_scale": 1.0 + 0.1 * jax.random.normal(ks[5], (A_WIDTH,), f32),
        "sgu_ln_g": 1.0 + 0.1 * jax.random.normal(ks[6], (B_WIDTH,), f32),
        "sgu_ln_b": 0.02 * jax.random.normal(ks[7], (B_WIDTH,), f32),
        "w_spatial": jax.random.normal(ks[8], (SGU_HEADS, CHUNK, CHUNK), f32) * CHUNK ** -0.5,
        "b_spatial": 1.0 + 0.1 * jax.random.normal(ks[9], (SGU_HEADS, CHUNK), f32),
        "w_out": jax.random.normal(ks[10], (MIX_WIDTH, D_MODEL), f32) * MIX_WIDTH ** -0.5,
        "final_g": 1.0 + 0.1 * jax.random.normal(ks[11], (D_MODEL,), f32),
    }


def reference(x_prompt, x_sample, norm_g, w_in, pool_w, pool_scale, sgu_ln_g, sgu_ln_b,
              w_spatial, b_spatial, w_out, final_g):
    y_prompt = trunk(x_prompt, norm_g, w_in, pool_w, pool_scale, sgu_ln_g, sgu_ln_b,
                     w_spatial, b_spatial, w_out, final_g)
    y_sample = trunk(x_sample, norm_g, w_in, pool_w, pool_scale, sgu_ln_g, sgu_ln_b,
                     w_spatial, b_spatial, w_out, final_g)
    return (y_prompt, y_sample)
```

```python
import functools

import jax
import jax.numpy as jnp
from jax import lax
from jax.experimental import pallas as pl
from jax.experimental.pallas import tpu as pltpu

D_MODEL = 1024
A_WIDTH = 512
B_WIDTH = 512
POOL_WINDOWS = (2, 4, 8, 16)
POOL_CH = 128
CHUNK = 128
SGU_HEADS = 4
SGU_HEAD_DIM = 128
IN_WIDTH = 2 * A_WIDTH + 3 * B_WIDTH
EPS = 1e-6

HALO = max(POOL_WINDOWS) // 2
TILE = 512
VMEM_LIMIT_BYTES = 56 * 1024 * 1024


def _rmsnorm(x, g):
    ms = jnp.mean(x * x, axis=-1, keepdims=True)
    return x * lax.rsqrt(ms + EPS) * g


def _silu(x):
    return x * (1.0 / (1.0 + jnp.exp(-x)))


def _layer_kernel(seq_len, tile,
                  x_ref, xprev_ref, xnext_ref, norm_g_ref, w_in_ref, pool_w_ref,
                  pool_scale_ref, ln_g_ref, ln_b_ref, w_s_ref, b_s_ref, w_out_ref,
                  final_g_ref, o_ref, a_scr):
    f32, bf16 = jnp.float32, jnp.bfloat16
    tiles_per_seq = seq_len // tile
    j = pl.program_id(0) % tiles_per_seq
    pos0 = j * tile

    x = x_ref[...]
    norm_g = norm_g_ref[...]
    h = _rmsnorm(x, norm_g).astype(bf16)
    p = jnp.dot(h, w_in_ref[...], preferred_element_type=f32)

    x_halo = jnp.concatenate([xprev_ref[...], xnext_ref[...]], axis=0)
    h_halo = _rmsnorm(x_halo, norm_g).astype(bf16)
    a_halo = jnp.dot(h_halo, w_in_ref[:, :A_WIDTH], preferred_element_type=f32)
    a_scr[0:HALO, :] = jnp.where(j > 0, a_halo[0:HALO], 0.0)
    a_scr[HALO:HALO + tile, :] = p[:, :A_WIDTH]
    a_scr[HALO + tile:, :] = jnp.where(j < tiles_per_seq - 1, a_halo[HALO:], 0.0)

    t_abs = pos0 + lax.broadcasted_iota(jnp.int32, (tile, 1), 0)
    y_a = []
    for g, w in enumerate(POOL_WINDOWS):
        r = w // 2
        cols = slice(g * POOL_CH, (g + 1) * POOL_CH)
        win = a_scr[pl.ds(HALO - r, tile), cols]
        for s in range(-r + 1, r):
            win = win + a_scr[pl.ds(HALO + s, tile), cols]
        cnt = (jnp.minimum(t_abs + r, seq_len) - jnp.maximum(t_abs - r, 0)).astype(f32)
        diff = win / cnt - p[:, cols]
        y = jnp.dot(diff.astype(bf16), pool_w_ref[g], preferred_element_type=f32)
        gate = p[:, A_WIDTH + g * POOL_CH:A_WIDTH + (g + 1) * POOL_CH]
        y_a.append(y * pool_scale_ref[:, cols] * _silu(gate))

    o = 2 * A_WIDTH
    u = p[:, o:o + B_WIDTH]
    v = p[:, o + B_WIDTH:o + 2 * B_WIDTH]
    gate_b = p[:, o + 2 * B_WIDTH:]
    mu = jnp.mean(v, axis=-1, keepdims=True)
    vc = v - mu
    var = jnp.mean(vc * vc, axis=-1, keepdims=True)
    vn = (vc * lax.rsqrt(var + EPS) * ln_g_ref[...] + ln_b_ref[...]).astype(bf16)
    mixed_rows = []
    for c in range(tile // CHUNK):
        heads = []
        for hd in range(SGU_HEADS):
            vn_ch = vn[c * CHUNK:(c + 1) * CHUNK, hd * SGU_HEAD_DIM:(hd + 1) * SGU_HEAD_DIM]
            heads.append(jnp.dot(w_s_ref[hd], vn_ch, preferred_element_type=f32)
                         + b_s_ref[hd])
        mixed_rows.append(jnp.concatenate(heads, axis=-1))
    mixed = jnp.concatenate(mixed_rows, axis=0)
    y_b = u * mixed * _silu(gate_b)

    mix = jnp.concatenate(y_a + [y_b], axis=-1).astype(bf16)
    xo = x + jnp.dot(mix, w_out_ref[...], preferred_element_type=f32)
    o_ref[...] = _rmsnorm(xo, final_g_ref[...])


def _layer(x, weights, tile):
    bsz, seq_len, d = x.shape
    assert d == D_MODEL and seq_len % tile == 0 and tile % CHUNK == 0
    n_rows = bsz * seq_len
    x2 = x.reshape(n_rows, d)
    halo_blocks_per_tile = tile // HALO
    n_halo_blocks = n_rows // HALO

    def const(shape):
        return pl.BlockSpec(shape, lambda i: (0,) * len(shape))

    in_specs = [
        pl.BlockSpec((tile, d), lambda i: (i, 0)),
        pl.BlockSpec((HALO, d), lambda i: (jnp.maximum(i * halo_blocks_per_tile - 1, 0), 0)),
        pl.BlockSpec((HALO, d), lambda i: (jnp.minimum((i + 1) * halo_blocks_per_tile,
                                                       n_halo_blocks - 1), 0)),
        const((1, D_MODEL)),
        const((D_MODEL, IN_WIDTH)),
        const((len(POOL_WINDOWS), POOL_CH, POOL_CH)),
        const((1, A_WIDTH)),
        const((1, B_WIDTH)),
        const((1, B_WIDTH)),
        const((SGU_HEADS, CHUNK, CHUNK)),
        const((SGU_HEADS, CHUNK, SGU_HEAD_DIM)),
        const((D_MODEL, D_MODEL)),
        const((1, D_MODEL)),
    ]
    out = pl.pallas_call(
        functools.partial(_layer_kernel, seq_len, tile),
        grid=(n_rows // tile,),
        in_specs=in_specs,
        out_specs=pl.BlockSpec((tile, d), lambda i: (i, 0)),
        out_shape=jax.ShapeDtypeStruct((n_rows, d), x.dtype),
        scratch_shapes=[pltpu.VMEM((tile + 2 * HALO, A_WIDTH), jnp.float32)],
        compiler_params=pltpu.CompilerParams(
            dimension_semantics=("arbitrary",),
            vmem_limit_bytes=VMEM_LIMIT_BYTES),
    )(x2, x2, x2, *weights)
    return out.reshape(bsz, seq_len, d)


def kernel(x_prompt, x_sample, norm_g, w_in, pool_w, pool_scale, sgu_ln_g, sgu_ln_b,
           w_spatial, b_spatial, w_out, final_g):
    bf16 = jnp.bfloat16
    weights = (
        norm_g.reshape(1, D_MODEL),
        w_in.astype(bf16),
        pool_w.astype(bf16),
        pool_scale.reshape(1, A_WIDTH),
        sgu_ln_g.reshape(1, B_WIDTH),
        sgu_ln_b.reshape(1, B_WIDTH),
        w_spatial.astype(bf16),
        jnp.broadcast_to(b_spatial[:, :, None], (SGU_HEADS, CHUNK, SGU_HEAD_DIM)),
        w_out.astype(bf16),
        final_g.reshape(1, D_MODEL),
    )
    return (_layer(x_prompt, weights, TILE), _layer(x_sample, weights, TILE))
```

```python
import functools

import jax
import jax.numpy as jnp
from jax import lax
from jax.experimental import pallas as pl
from jax.experimental.pallas import tpu as pltpu

D_MODEL = 1024
A_WIDTH = 512
B_WIDTH = 512
POOL_WINDOWS = (2, 4, 8, 16)
POOL_CH = 128
CHUNK = 128
SGU_HEADS = 4
SGU_HEAD_DIM = 128
IN_WIDTH = 2 * A_WIDTH + 3 * B_WIDTH
REST_WIDTH = IN_WIDTH - A_WIDTH
EPS = 1e-6

HALO = max(POOL_WINDOWS) // 2
BF16_ROWS = 16
MXU_COLS = 256
MM_ROWS = 256
TILE = 512
VMEM_LIMIT_BYTES = 56 * 1024 * 1024


def _silu(x):
    return x * (1.0 / (1.0 + jnp.exp(-x)))


class _Step:
    def __init__(self, seq_len, tile, refs, scratch, par_norm, par_proj, par_out):
        (self.xin, self.xprev, self.xnext, self.xout, self.norm_g, self.w_in, self.pool_w,
         self.pool_scale, self.ln_g, self.ln_b, self.w_s, self.b_s, self.w_out,
         self.final_g, self.o) = refs
        h_bufs, a_bufs, r_bufs, self.mix, self.ss, self.vn = scratch
        self.h_norm = h_bufs[par_norm]
        self.h_proj, self.a_proj, self.r_proj = h_bufs[par_proj], a_bufs[par_proj], r_bufs[par_proj]
        self.a_out, self.r_out = a_bufs[par_out], r_bufs[par_out]
        self.seq_len, self.tile = seq_len, tile
        self.tiles_per_seq = seq_len // tile
        s = pl.program_id(0)
        self.j_proj = (s - 1) % self.tiles_per_seq
        self.j_out = (s - 2) % self.tiles_per_seq

    def _rmsnorm_to_h(self, x, rows):
        ms = jnp.mean(x * x, axis=-1, keepdims=True)
        h = x * lax.rsqrt(ms + EPS) * self.norm_g[...]
        self.h_norm[rows, :] = h.astype(self.h_norm.dtype)

    def norm_halo(self):
        x_halo = jnp.concatenate([self.xprev[...], self.xnext[...]], axis=0)
        self._rmsnorm_to_h(x_halo, pl.ds(0, BF16_ROWS))

    def norm_chunk(self, c):
        self._rmsnorm_to_h(self.xin[pl.ds(c * CHUNK, CHUNK), :],
                           pl.ds(BF16_ROWS + c * CHUNK, CHUNK))

    def proj_piece(self, n, rb):
        f32 = jnp.float32
        cols = pl.ds(n * MXU_COLS, MXU_COLS)
        row0 = rb * MM_ROWS
        p = jnp.dot(self.h_proj[pl.ds(BF16_ROWS + row0, MM_ROWS), :], self.w_in[:, cols],
                    preferred_element_type=f32)
        if n * MXU_COLS < A_WIDTH:
            self.a_proj[pl.ds(HALO + row0, MM_ROWS), cols] = p
        else:
            self.r_proj[pl.ds(row0, MM_ROWS), pl.ds(n * MXU_COLS - A_WIDTH, MXU_COLS)] = p

    def proj_halo(self):
        a_halo = jnp.dot(self.h_proj[0:BF16_ROWS, :], self.w_in[:, 0:A_WIDTH],
                         preferred_element_type=jnp.float32)
        self.a_proj[0:HALO, :] = jnp.where(self.j_proj > 0, a_halo[0:HALO], 0.0)
        self.a_proj[HALO + self.tile:, :] = jnp.where(
            self.j_proj < self.tiles_per_seq - 1, a_halo[HALO:], 0.0)

    def pool_piece(self, c, g):
        f32, bf16 = jnp.float32, jnp.bfloat16
        row0 = c * CHUNK
        rows = pl.ds(row0, CHUNK)
        a_scr = self.a_out
        r = POOL_WINDOWS[g] // 2
        cols = pl.ds(g * POOL_CH, POOL_CH)
        t_abs = self.j_out * self.tile + row0 + lax.broadcasted_iota(jnp.int32, (CHUNK, 1), 0)
        n_ext = CHUNK + 2 * HALO
        ext = a_scr[pl.ds(row0, n_ext), cols]
        run, span = ext, 1
        while span < r:
            run = run + pltpu.roll(run, n_ext - span, 0)
            span *= 2
        win = (run + pltpu.roll(run, r, 0))[HALO:HALO + CHUNK]
        cnt = (jnp.minimum(t_abs + r, self.seq_len) - jnp.maximum(t_abs - r, 0)).astype(f32)
        diff = win / cnt - ext[HALO:HALO + CHUNK]
        y = jnp.dot(diff.astype(bf16), self.pool_w[g], preferred_element_type=f32)
        y_a = y * self.pool_scale[:, cols] * _silu(self.r_out[rows, cols])
        self.mix[rows, cols] = y_a.astype(bf16)

    def ln_piece(self, c):
        v = self.r_out[pl.ds(c * CHUNK, CHUNK), pl.ds(A_WIDTH + B_WIDTH, B_WIDTH)]
        mu = jnp.mean(v, axis=-1, keepdims=True)
        vc = v - mu
        var = jnp.mean(vc * vc, axis=-1, keepdims=True)
        vn = vc * lax.rsqrt(var + EPS) * self.ln_g[...] + self.ln_b[...]
        self.vn[c] = vn.astype(self.vn.dtype)

    def sgu_piece(self, c, hd):
        rows = pl.ds(c * CHUNK, CHUNK)
        hcols = pl.ds(hd * SGU_HEAD_DIM, SGU_HEAD_DIM)
        mixed = jnp.dot(self.w_s[hd], self.vn[c, :, hcols],
                        preferred_element_type=jnp.float32) + self.b_s[hd]
        u = self.r_out[rows, pl.ds(A_WIDTH + hd * SGU_HEAD_DIM, SGU_HEAD_DIM)]
        gate_b = self.r_out[rows, pl.ds(A_WIDTH + 2 * B_WIDTH + hd * SGU_HEAD_DIM, SGU_HEAD_DIM)]
        y_b = u * mixed * _silu(gate_b)
        self.mix[rows, pl.ds(A_WIDTH + hd * SGU_HEAD_DIM, SGU_HEAD_DIM)] = y_b.astype(self.mix.dtype)

    def outproj_piece(self, m, rb):
        cols = pl.ds(m * MXU_COLS, MXU_COLS)
        rows = pl.ds(rb * MM_ROWS, MM_ROWS)
        xo = self.xout[rows, cols] + jnp.dot(self.mix[rows, :], self.w_out[:, cols],
                                             preferred_element_type=jnp.float32)
        self.o[rows, cols] = xo
        part = jnp.sum(xo * xo, axis=-1, keepdims=True)
        if m == 0:
            self.ss[rows, :] = part
        else:
            self.ss[rows, :] += part

    def final_piece(self, c):
        rows = pl.ds(c * CHUNK, CHUNK)
        scale = lax.rsqrt(self.ss[rows, :] * (1.0 / D_MODEL) + EPS)
        self.o[rows, :] = self.o[rows, :] * scale * self.final_g[...]


def _emit_balanced(chain, fillers):
    def is_mxu(piece):
        return piece[1] >= piece[2]

    chain = list(chain)
    mxu_fill = [p for p in fillers if is_mxu(p)]
    valu_fill = [p for p in fillers if not is_mxu(p)]
    tot_m = sum(p[1] for p in chain + fillers) or 1
    tot_v = sum(p[2] for p in chain + fillers) or 1
    done_m = done_v = 0.0
    while chain or mxu_fill or valu_fill:
        want_mxu = done_m / tot_m <= done_v / tot_v
        head_fits = bool(chain) and is_mxu(chain[0]) == want_mxu
        order = [mxu_fill, chain, valu_fill] if want_mxu else [valu_fill, chain, mxu_fill]
        if head_fits:
            order.insert(0, chain)
        pick = next(lst for lst in order if lst)
        fn, m, v = pick.pop(0)
        fn()
        done_m += m
        done_v += v


def _layer_kernel(seq_len, tile, n_tiles, *refs):
    io_refs, scr = refs[:15], refs[15:]
    scratch = ((scr[0], scr[1]), (scr[2], scr[3]), (scr[4], scr[5]), scr[6], scr[7], scr[8])
    n_chunks = tile // CHUNK
    n_row_blocks = tile // MM_ROWS
    s = pl.program_id(0)
    P = functools.partial

    def step(par_norm, par_proj, par_out):
        return _Step(seq_len, tile, io_refs, scratch, par_norm, par_proj, par_out)

    def norm_pieces(st):
        return [(P(st.norm_chunk, c), 0, 160) for c in range(n_chunks)] + [(st.norm_halo, 0, 30)]

    def proj_pieces(st):
        return ([(P(st.proj_piece, n, rb), 256, 20)
                 for n in range(IN_WIDTH // MXU_COLS) for rb in range(n_row_blocks)]
                + [(st.proj_halo, 64, 10)])

    def out_pieces(st):
        pieces = []
        for c in range(n_chunks):
            pieces.append((P(st.ln_piece, c), 0, 130))
            for g, w in enumerate(POOL_WINDOWS):
                pieces.append((P(st.pool_piece, c, g), 32, 50 + 12 * w.bit_length()))
            for hd in range(SGU_HEADS):
                pieces.append((P(st.sgu_piece, c, hd), 32, 40))
        for m in range(D_MODEL // MXU_COLS):
            for rb in range(n_row_blocks):
                pieces.append((P(st.outproj_piece, m, rb), 256, 50))
        for c in range(n_chunks):
            pieces.append((P(st.final_piece, c), 0, 100))
        return pieces

    @pl.when(s == 0)
    def _():
        _emit_balanced([], norm_pieces(step(0, 0, 0)))

    @pl.when(s == 1)
    def _():
        st = step(1, 0, 0)
        _emit_balanced([], proj_pieces(st) + norm_pieces(st))

    for par in (0, 1):
        @pl.when((s >= 2) & (s < n_tiles) & (s % 2 == par))
        def _():
            st = step(par, 1 - par, par)
            _emit_balanced(out_pieces(st), proj_pieces(st) + norm_pieces(st))

    @pl.when(s == n_tiles)
    def _():
        st = step(0, (n_tiles - 1) % 2, n_tiles % 2)
        _emit_balanced(out_pieces(st), proj_pieces(st))

    @pl.when(s == n_tiles + 1)
    def _():
        _emit_balanced(out_pieces(step(0, 0, (n_tiles - 1) % 2)), [])


def _layer(x, weights, tile):
    bsz, seq_len, d = x.shape
    assert d == D_MODEL and seq_len % tile == 0 and tile % MM_ROWS == 0 and tile % CHUNK == 0
    n_rows = bsz * seq_len
    n_tiles = n_rows // tile
    assert n_tiles >= 2
    x2 = x.reshape(n_rows, d)
    halo_blocks_per_tile = tile // HALO
    n_halo_blocks = n_rows // HALO

    def const(shape):
        return pl.BlockSpec(shape, lambda s: (0,) * len(shape))

    def tile_in(s):
        return jnp.minimum(s, n_tiles - 1)

    def tile_out(s):
        return jnp.maximum(s - 2, 0)

    in_specs = [
        pl.BlockSpec((tile, d), lambda s: (tile_in(s), 0)),
        pl.BlockSpec((HALO, d),
                     lambda s: (jnp.maximum(tile_in(s) * halo_blocks_per_tile - 1, 0), 0)),
        pl.BlockSpec((HALO, d),
                     lambda s: (jnp.minimum((tile_in(s) + 1) * halo_blocks_per_tile,
                                            n_halo_blocks - 1), 0)),
        pl.BlockSpec((tile, d), lambda s: (tile_out(s), 0)),
        const((1, D_MODEL)),
        const((D_MODEL, IN_WIDTH)),
        const((len(POOL_WINDOWS), POOL_CH, POOL_CH)),
        const((1, A_WIDTH)),
        const((1, B_WIDTH)),
        const((1, B_WIDTH)),
        const((SGU_HEADS, CHUNK, CHUNK)),
        const((SGU_HEADS, CHUNK, SGU_HEAD_DIM)),
        const((D_MODEL, D_MODEL)),
        const((1, D_MODEL)),
    ]
    h_shape = pltpu.VMEM((BF16_ROWS + tile, D_MODEL), jnp.bfloat16)
    a_shape = pltpu.VMEM((tile + 2 * HALO, A_WIDTH), jnp.float32)
    r_shape = pltpu.VMEM((tile, REST_WIDTH), jnp.float32)
    mix_shape = pltpu.VMEM((tile, D_MODEL), jnp.bfloat16)
    ss_shape = pltpu.VMEM((tile, 1), jnp.float32)
    vn_shape = pltpu.VMEM((tile // CHUNK, CHUNK, B_WIDTH), jnp.bfloat16)
    out = pl.pallas_call(
        functools.partial(_layer_kernel, seq_len, tile, n_tiles),
        grid=(n_tiles + 2,),
        in_specs=in_specs,
        out_specs=pl.BlockSpec((tile, d), lambda s: (tile_out(s), 0)),
        out_shape=jax.ShapeDtypeStruct((n_rows, d), x.dtype),
        scratch_shapes=[h_shape, h_shape, a_shape, a_shape, r_shape, r_shape,
                        mix_shape, ss_shape, vn_shape],
        compiler_params=pltpu.CompilerParams(
            dimension_semantics=("arbitrary",),
            vmem_limit_bytes=VMEM_LIMIT_BYTES),
    )(x2, x2, x2, x2, *weights)
    return out.reshape(bsz, seq_len, d)


def kernel(x_prompt, x_sample, norm_g, w_in, pool_w, pool_scale, sgu_ln_g, sgu_ln_b,
           w_spatial, b_spatial, w_out, final_g):
    bf16 = jnp.bfloat16
    weights = (
        norm_g.reshape(1, D_MODEL),
        w_in.astype(bf16),
        pool_w.astype(bf16),
        pool_scale.reshape(1, A_WIDTH),
        sgu_ln_g.reshape(1, B_WIDTH),
        sgu_ln_b.reshape(1, B_WIDTH),
        w_spatial.astype(bf16),
        jnp.broadcast_to(b_spatial[:, :, None], (SGU_HEADS, CHUNK, SGU_HEAD_DIM)),
        w_out.astype(bf16),
        final_g.reshape(1, D_MODEL),
    )
    return (_layer(x_prompt, weights, TILE), _layer(x_sample, weights, TILE))
```

```python
import functools

import jax
import jax.numpy as jnp
from jax import lax
from jax.experimental import pallas as pl
from jax.experimental.pallas import tpu as pltpu

D_MODEL = 1024
A_WIDTH = 512
B_WIDTH = 512
POOL_WINDOWS = (2, 4, 8, 16)
POOL_CH = 128
CHUNK = 128
SGU_HEADS = 4
SGU_HEAD_DIM = 128
IN_WIDTH = 2 * A_WIDTH + 3 * B_WIDTH
REST_WIDTH = IN_WIDTH - A_WIDTH
EPS = 1e-6

HALO = max(POOL_WINDOWS) // 2
BF16_ROWS = 16
MXU_COLS = 256
MM_ROWS = 512
TILE = 512
VMEM_LIMIT_BYTES = 56 * 1024 * 1024


def _silu(x):
    return x * (1.0 / (1.0 + jnp.exp(-x)))


class _Step:
    def __init__(self, seq_len, tile, refs, scratch, par_norm, par_proj, par_out):
        (self.xin, self.xprev, self.xnext, self.xout, self.norm_g, self.w_in, self.pool_w,
         self.pool_scale, self.ln_g, self.ln_b, self.w_s, self.b_s, self.w_out,
         self.final_g, self.o) = refs
        h_bufs, a_bufs, r_bufs, self.mix, self.ss, self.vn = scratch
        self.h_norm = h_bufs[par_norm]
        self.h_proj, self.a_proj, self.r_proj = h_bufs[par_proj], a_bufs[par_proj], r_bufs[par_proj]
        self.a_out, self.r_out = a_bufs[par_out], r_bufs[par_out]
        self.seq_len, self.tile = seq_len, tile
        self.tiles_per_seq = seq_len // tile
        s = pl.program_id(0)
        self.j_proj = (s - 1) % self.tiles_per_seq
        self.j_out = (s - 2) % self.tiles_per_seq

    def _rmsnorm_to_h(self, x, rows):
        ms = jnp.mean(x * x, axis=-1, keepdims=True)
        h = x * lax.rsqrt(ms + EPS) * self.norm_g[...]
        self.h_norm[rows, :] = h.astype(self.h_norm.dtype)

    def norm_halo(self):
        x_halo = jnp.concatenate([self.xprev[...], self.xnext[...]], axis=0)
        self._rmsnorm_to_h(x_halo, pl.ds(0, BF16_ROWS))

    def norm_chunk(self, c):
        self._rmsnorm_to_h(self.xin[pl.ds(c * CHUNK, CHUNK), :],
                           pl.ds(BF16_ROWS + c * CHUNK, CHUNK))

    def proj_piece(self, n, rb):
        f32 = jnp.float32
        cols = pl.ds(n * MXU_COLS, MXU_COLS)
        row0 = rb * MM_ROWS
        p = jnp.dot(self.h_proj[pl.ds(BF16_ROWS + row0, MM_ROWS), :], self.w_in[:, cols],
                    preferred_element_type=f32)
        if n * MXU_COLS < A_WIDTH:
            self.a_proj[pl.ds(HALO + row0, MM_ROWS), cols] = p
        else:
            self.r_proj[pl.ds(row0, MM_ROWS), pl.ds(n * MXU_COLS - A_WIDTH, MXU_COLS)] = p

    def proj_halo(self):
        a_halo = jnp.dot(self.h_proj[0:BF16_ROWS, :], self.w_in[:, 0:A_WIDTH],
                         preferred_element_type=jnp.float32)
        self.a_proj[0:HALO, :] = jnp.where(self.j_proj > 0, a_halo[0:HALO], 0.0)
        self.a_proj[HALO + self.tile:, :] = jnp.where(
            self.j_proj < self.tiles_per_seq - 1, a_halo[HALO:], 0.0)

    def pool_piece(self, c, g):
        f32, bf16 = jnp.float32, jnp.bfloat16
        row0 = c * CHUNK
        rows = pl.ds(row0, CHUNK)
        a_scr = self.a_out
        r = POOL_WINDOWS[g] // 2
        cols = pl.ds(g * POOL_CH, POOL_CH)
        t_abs = self.j_out * self.tile + row0 + lax.broadcasted_iota(jnp.int32, (CHUNK, 1), 0)
        n_ext = CHUNK + 2 * HALO
        ext = a_scr[pl.ds(row0, n_ext), cols]
        run, span = ext, 1
        while span < r:
            run = run + pltpu.roll(run, n_ext - span, 0)
            span *= 2
        win = (run + pltpu.roll(run, r, 0))[HALO:HALO + CHUNK]
        cnt = (jnp.minimum(t_abs + r, self.seq_len) - jnp.maximum(t_abs - r, 0)).astype(f32)
        diff = win / cnt - ext[HALO:HALO + CHUNK]
        y = jnp.dot(diff.astype(bf16), self.pool_w[g], preferred_element_type=f32)
        y_a = y * self.pool_scale[:, cols] * _silu(self.r_out[rows, cols])
        self.mix[rows, cols] = y_a.astype(bf16)

    def ln_piece(self, c):
        v = self.r_out[pl.ds(c * CHUNK, CHUNK), pl.ds(A_WIDTH + B_WIDTH, B_WIDTH)]
        mu = jnp.mean(v, axis=-1, keepdims=True)
        vc = v - mu
        var = jnp.mean(vc * vc, axis=-1, keepdims=True)
        vn = vc * lax.rsqrt(var + EPS) * self.ln_g[...] + self.ln_b[...]
        self.vn[c] = vn.astype(self.vn.dtype)

    def sgu_piece(self, c, hd):
        rows = pl.ds(c * CHUNK, CHUNK)
        hcols = pl.ds(hd * SGU_HEAD_DIM, SGU_HEAD_DIM)
        mixed = jnp.dot(self.w_s[hd], self.vn[c, :, hcols],
                        preferred_element_type=jnp.float32) + self.b_s[hd]
        u = self.r_out[rows, pl.ds(A_WIDTH + hd * SGU_HEAD_DIM, SGU_HEAD_DIM)]
        gate_b = self.r_out[rows, pl.ds(A_WIDTH + 2 * B_WIDTH + hd * SGU_HEAD_DIM, SGU_HEAD_DIM)]
        y_b = u * mixed * _silu(gate_b)
        self.mix[rows, pl.ds(A_WIDTH + hd * SGU_HEAD_DIM, SGU_HEAD_DIM)] = y_b.astype(self.mix.dtype)

    def outproj_piece(self, m, rb):
        cols = pl.ds(m * MXU_COLS, MXU_COLS)
        rows = pl.ds(rb * MM_ROWS, MM_ROWS)
        xo = self.xout[rows, cols] + jnp.dot(self.mix[rows, :], self.w_out[:, cols],
                                             preferred_element_type=jnp.float32)
        self.o[rows, cols] = xo
        part = jnp.sum(xo * xo, axis=-1, keepdims=True)
        if m == 0:
            self.ss[rows, :] = part
        else:
            self.ss[rows, :] += part

    def final_piece(self, c):
        rows = pl.ds(c * CHUNK, CHUNK)
        scale = lax.rsqrt(self.ss[rows, :] * (1.0 / D_MODEL) + EPS)
        self.o[rows, :] = self.o[rows, :] * scale * self.final_g[...]


def _emit_balanced(chain, fillers):
    def is_mxu(piece):
        return piece[1] >= piece[2]

    chain = list(chain)
    mxu_fill = [p for p in fillers if is_mxu(p)]
    valu_fill = [p for p in fillers if not is_mxu(p)]
    tot_m = sum(p[1] for p in chain + fillers) or 1
    tot_v = sum(p[2] for p in chain + fillers) or 1
    done_m = done_v = 0.0
    while chain or mxu_fill or valu_fill:
        want_mxu = done_m / tot_m <= done_v / tot_v
        head_fits = bool(chain) and is_mxu(chain[0]) == want_mxu
        order = [mxu_fill, chain, valu_fill] if want_mxu else [valu_fill, chain, mxu_fill]
        if head_fits:
            order.insert(0, chain)
        pick = next(lst for lst in order if lst)
        fn, m, v = pick.pop(0)
        fn()
        done_m += m
        done_v += v


def _layer_kernel(seq_len, tile, n_tiles, *refs):
    io_refs, scr = refs[:15], refs[15:]
    scratch = ((scr[0], scr[1]), (scr[2], scr[3]), (scr[4], scr[5]), scr[6], scr[7], scr[8])
    n_chunks = tile // CHUNK
    n_row_blocks = tile // MM_ROWS
    s = pl.program_id(0)
    P = functools.partial

    def step(par_norm, par_proj, par_out):
        return _Step(seq_len, tile, io_refs, scratch, par_norm, par_proj, par_out)

    def norm_pieces(st):
        return [(P(st.norm_chunk, c), 0, 160) for c in range(n_chunks)] + [(st.norm_halo, 0, 30)]

    def proj_pieces(st):
        return ([(P(st.proj_piece, n, rb), MM_ROWS, MM_ROWS // 12)
                 for n in range(IN_WIDTH // MXU_COLS) for rb in range(n_row_blocks)]
                + [(st.proj_halo, 64, 10)])

    def out_pieces(st):
        pieces = []
        for c in range(n_chunks):
            pieces.append((P(st.ln_piece, c), 0, 130))
            for g, w in enumerate(POOL_WINDOWS):
                pieces.append((P(st.pool_piece, c, g), 32, 50 + 12 * w.bit_length()))
            for hd in range(SGU_HEADS):
                pieces.append((P(st.sgu_piece, c, hd), 32, 40))
        for m in range(D_MODEL // MXU_COLS):
            for rb in range(n_row_blocks):
                pieces.append((P(st.outproj_piece, m, rb), MM_ROWS, MM_ROWS // 5))
        for c in range(n_chunks):
            pieces.append((P(st.final_piece, c), 0, 100))
        return pieces

    @pl.when(s == 0)
    def _():
        _emit_balanced([], norm_pieces(step(0, 0, 0)))

    @pl.when(s == 1)
    def _():
        st = step(1, 0, 0)
        _emit_balanced([], proj_pieces(st) + norm_pieces(st))

    for par in (0, 1):
        @pl.when((s >= 2) & (s < n_tiles) & (s % 2 == par))
        def _():
            st = step(par, 1 - par, par)
            _emit_balanced(out_pieces(st), proj_pieces(st) + norm_pieces(st))

    @pl.when(s == n_tiles)
    def _():
        st = step(0, (n_tiles - 1) % 2, n_tiles % 2)
        _emit_balanced(out_pieces(st), proj_pieces(st))

    @pl.when(s == n_tiles + 1)
    def _():
        _emit_balanced(out_pieces(step(0, 0, (n_tiles - 1) % 2)), [])


def _layer(x, weights, tile):
    bsz, seq_len, d = x.shape
    assert d == D_MODEL and seq_len % tile == 0 and tile % MM_ROWS == 0 and tile % CHUNK == 0
    n_rows = bsz * seq_len
    n_tiles = n_rows // tile
    assert n_tiles >= 2
    x2 = x.reshape(n_rows, d)
    halo_blocks_per_tile = tile // HALO
    n_halo_blocks = n_rows // HALO

    def const(shape):
        return pl.BlockSpec(shape, lambda s: (0,) * len(shape))

    def tile_in(s):
        return jnp.minimum(s, n_tiles - 1)

    def tile_out(s):
        return jnp.maximum(s - 2, 0)

    in_specs = [
        pl.BlockSpec((tile, d), lambda s: (tile_in(s), 0)),
        pl.BlockSpec((HALO, d),
                     lambda s: (jnp.maximum(tile_in(s) * halo_blocks_per_tile - 1, 0), 0)),
        pl.BlockSpec((HALO, d),
                     lambda s: (jnp.minimum((tile_in(s) + 1) * halo_blocks_per_tile,
                                            n_halo_blocks - 1), 0)),
        pl.BlockSpec((tile, d), lambda s: (tile_out(s), 0)),
        const((1, D_MODEL)),
        const((D_MODEL, IN_WIDTH)),
        const((len(POOL_WINDOWS), POOL_CH, POOL_CH)),
        const((1, A_WIDTH)),
        const((1, B_WIDTH)),
        const((1, B_WIDTH)),
        const((SGU_HEADS, CHUNK, CHUNK)),
        const((SGU_HEADS, CHUNK, SGU_HEAD_DIM)),
        const((D_MODEL, D_MODEL)),
        const((1, D_MODEL)),
    ]
    h_shape = pltpu.VMEM((BF16_ROWS + tile, D_MODEL), jnp.bfloat16)
    a_shape = pltpu.VMEM((tile + 2 * HALO, A_WIDTH), jnp.float32)
    r_shape = pltpu.VMEM((tile, REST_WIDTH), jnp.float32)
    mix_shape = pltpu.VMEM((tile, D_MODEL), jnp.bfloat16)
    ss_shape = pltpu.VMEM((tile, 1), jnp.float32)
    vn_shape = pltpu.VMEM((tile // CHUNK, CHUNK, B_WIDTH), jnp.bfloat16)
    out = pl.pallas_call(
        functools.partial(_layer_kernel, seq_len, tile, n_tiles),
        grid=(n_tiles + 2,),
        in_specs=in_specs,
        out_specs=pl.BlockSpec((tile, d), lambda s: (tile_out(s), 0)),
        out_shape=jax.ShapeDtypeStruct((n_rows, d), x.dtype),
        scratch_shapes=[h_shape, h_shape, a_shape, a_shape, r_shape, r_shape,
                        mix_shape, ss_shape, vn_shape],
        compiler_params=pltpu.CompilerParams(
            dimension_semantics=("arbitrary",),
            vmem_limit_bytes=VMEM_LIMIT_BYTES),
    )(x2, x2, x2, x2, *weights)
    return out.reshape(bsz, seq_len, d)


def kernel(x_prompt, x_sample, norm_g, w_in, pool_w, pool_scale, sgu_ln_g, sgu_ln_b,
           w_spatial, b_spatial, w_out, final_g):
    bf16 = jnp.bfloat16
    weights = (
        norm_g.reshape(1, D_MODEL),
        w_in.astype(bf16),
        pool_w.astype(bf16),
        pool_scale.reshape(1, A_WIDTH),
        sgu_ln_g.reshape(1, B_WIDTH),
        sgu_ln_b.reshape(1, B_WIDTH),
        w_spatial.astype(bf16),
        jnp.broadcast_to(b_spatial[:, :, None], (SGU_HEADS, CHUNK, SGU_HEAD_DIM)),
        w_out.astype(bf16),
        final_g.reshape(1, D_MODEL),
    )
    return (_layer(x_prompt, weights, TILE), _layer(x_sample, weights, TILE))
```

```python
import functools

import jax
import jax.numpy as jnp
from jax import lax
from jax.experimental import pallas as pl
from jax.experimental.pallas import tpu as pltpu

D_MODEL = 1024
A_WIDTH = 512
B_WIDTH = 512
POOL_WINDOWS = (2, 4, 8, 16)
POOL_CH = 128
CHUNK = 128
SGU_HEADS = 4
SGU_HEAD_DIM = 128
IN_WIDTH = 2 * A_WIDTH + 3 * B_WIDTH
EPS = 1e-6

COL_A, COL_GATE_A, COL_U, COL_V, COL_GATE_B = 0, 512, 1024, 1536, 2048

HALO = max(POOL_WINDOWS) // 2
MXU_COLS = 256
TILE = 1024
OUT_ROWS = 256
VMEM_LIMIT_BYTES = 56 * 1024 * 1024


def _rmsnorm(x, g):
    ms = jnp.mean(x * x, axis=-1, keepdims=True)
    return x * lax.rsqrt(ms + EPS) * g


def _silu(x):
    return x * (1.0 / (1.0 + jnp.exp(-x)))


def _layer_kernel(seq_len, tile,
                  x_ref, xprev_ref, xnext_ref, norm_g_ref, w_in_ref, pool_w2_ref,
                  pool_scale_ref, ln_g_ref, ln_b_ref, w_s_ref, b_s_ref, w_out_ref,
                  final_g_ref, o_ref, a_scr, mix_scr):
    f32, bf16 = jnp.float32, jnp.bfloat16
    tiles_per_seq = seq_len // tile
    n_chunks = tile // CHUNK
    j = pl.program_id(0) % tiles_per_seq
    pos0 = j * tile

    def proj(hh, col0, width):
        return jnp.dot(hh, w_in_ref[:, col0:col0 + width], preferred_element_type=f32)

    norm_g = norm_g_ref[...]
    h = _rmsnorm(x_ref[...], norm_g).astype(bf16)

    x_halo = jnp.concatenate([xprev_ref[...], xnext_ref[...]], axis=0)
    a_halo = proj(_rmsnorm(x_halo, norm_g).astype(bf16), COL_A, A_WIDTH)
    a_scr[0:HALO, :] = jnp.where(j > 0, a_halo[0:HALO], 0.0)
    a_scr[HALO:HALO + tile, :] = proj(h, COL_A, A_WIDTH)
    a_scr[HALO + tile:, :] = jnp.where(j < tiles_per_seq - 1, a_halo[HALO:], 0.0)

    t_abs = pos0 + lax.broadcasted_iota(jnp.int32, (tile, 1), 0)
    n_ext = tile + 2 * HALO
    diffs = []
    for g, w in enumerate(POOL_WINDOWS):
        r = w // 2
        ext = a_scr[:, g * POOL_CH:(g + 1) * POOL_CH]
        run, span = ext, 1
        while span < r:
            run = run + pltpu.roll(run, n_ext - span, 0)
            span *= 2
        win = (run + pltpu.roll(run, r, 0))[HALO:HALO + tile]
        cnt = (jnp.minimum(t_abs + r, seq_len) - jnp.maximum(t_abs - r, 0)).astype(f32)
        diffs.append((win / cnt - ext[HALO:HALO + tile]).astype(bf16))

    gate_a = proj(h, COL_GATE_A, A_WIDTH)
    v = proj(h, COL_V, B_WIDTH)
    silu_a = _silu(gate_a)
    for pair in range(len(POOL_WINDOWS) // 2):
        cols = slice(pair * MXU_COLS, (pair + 1) * MXU_COLS)
        diff2 = jnp.concatenate(diffs[2 * pair:2 * pair + 2], axis=-1)
        y = jnp.dot(diff2, pool_w2_ref[pair], preferred_element_type=f32)
        mix_scr[:, cols] = (y * pool_scale_ref[:, cols] * silu_a[:, cols]).astype(bf16)

    gate_b = proj(h, COL_GATE_B, B_WIDTH)
    mu = jnp.mean(v, axis=-1, keepdims=True)
    vc = v - mu
    var = jnp.mean(vc * vc, axis=-1, keepdims=True)
    vn = (vc * lax.rsqrt(var + EPS) * ln_g_ref[...] + ln_b_ref[...]).astype(bf16)
    mixed_heads = []
    for hd in range(SGU_HEADS):
        hcols = slice(hd * SGU_HEAD_DIM, (hd + 1) * SGU_HEAD_DIM)
        vn_wide = jnp.concatenate(
            [vn[c * CHUNK:(c + 1) * CHUNK, hcols] for c in range(n_chunks)], axis=-1)
        mw = jnp.dot(w_s_ref[hd], vn_wide, preferred_element_type=f32)
        bias = b_s_ref[hd]
        mixed_heads.append(jnp.concatenate(
            [mw[:, c * SGU_HEAD_DIM:(c + 1) * SGU_HEAD_DIM] + bias for c in range(n_chunks)],
            axis=0))
    mixed = jnp.concatenate(mixed_heads, axis=-1)

    u = proj(h, COL_U, B_WIDTH)
    gated = mixed * _silu(gate_b)
    mix_scr[:, A_WIDTH:] = (u * gated).astype(bf16)

    final_g = final_g_ref[...]
    for rb in range(tile // OUT_ROWS):
        rows = slice(rb * OUT_ROWS, (rb + 1) * OUT_ROWS)
        xo = x_ref[rows, :] + jnp.dot(mix_scr[rows, :], w_out_ref[...],
                                      preferred_element_type=f32)
        o_ref[rows, :] = _rmsnorm(xo, final_g)


def _layer(x, weights, tile):
    bsz, seq_len, d = x.shape
    assert d == D_MODEL and seq_len % tile == 0 and tile % CHUNK == 0 and tile % OUT_ROWS == 0
    n_rows = bsz * seq_len
    x2 = x.reshape(n_rows, d)
    halo_blocks_per_tile = tile // HALO
    n_halo_blocks = n_rows // HALO

    def const(shape):
        return pl.BlockSpec(shape, lambda i: (0,) * len(shape))

    in_specs = [
        pl.BlockSpec((tile, d), lambda i: (i, 0)),
        pl.BlockSpec((HALO, d), lambda i: (jnp.maximum(i * halo_blocks_per_tile - 1, 0), 0)),
        pl.BlockSpec((HALO, d), lambda i: (jnp.minimum((i + 1) * halo_blocks_per_tile,
                                                       n_halo_blocks - 1), 0)),
        const((1, D_MODEL)),
        const((D_MODEL, IN_WIDTH)),
        const((len(POOL_WINDOWS) // 2, MXU_COLS, MXU_COLS)),
        const((1, A_WIDTH)),
        const((1, B_WIDTH)),
        const((1, B_WIDTH)),
        const((SGU_HEADS, CHUNK, CHUNK)),
        const((SGU_HEADS, CHUNK, SGU_HEAD_DIM)),
        const((D_MODEL, D_MODEL)),
        const((1, D_MODEL)),
    ]
    out = pl.pallas_call(
        functools.partial(_layer_kernel, seq_len, tile),
        grid=(n_rows // tile,),
        in_specs=in_specs,
        out_specs=pl.BlockSpec((tile, d), lambda i: (i, 0)),
        out_shape=jax.ShapeDtypeStruct((n_rows, d), x.dtype),
        scratch_shapes=[pltpu.VMEM((tile + 2 * HALO, A_WIDTH), jnp.float32),
                        pltpu.VMEM((tile, D_MODEL), jnp.bfloat16)],
        compiler_params=pltpu.CompilerParams(
            dimension_semantics=("arbitrary",),
            vmem_limit_bytes=VMEM_LIMIT_BYTES),
    )(x2, x2, x2, *weights)
    return out.reshape(bsz, seq_len, d)


def _pair_block_diagonal(pool_w):
    g, c, _ = pool_w.shape
    z = jnp.zeros((g // 2, c, c), pool_w.dtype)
    top = jnp.concatenate([pool_w[0::2], z], axis=-1)
    bot = jnp.concatenate([z, pool_w[1::2]], axis=-1)
    return jnp.concatenate([top, bot], axis=-2)


def kernel(x_prompt, x_sample, norm_g, w_in, pool_w, pool_scale, sgu_ln_g, sgu_ln_b,
           w_spatial, b_spatial, w_out, final_g):
    bf16 = jnp.bfloat16
    weights = (
        norm_g.reshape(1, D_MODEL),
        w_in.astype(bf16),
        _pair_block_diagonal(pool_w.astype(bf16)),
        pool_scale.reshape(1, A_WIDTH),
        sgu_ln_g.reshape(1, B_WIDTH),
        sgu_ln_b.reshape(1, B_WIDTH),
        w_spatial.astype(bf16),
        jnp.broadcast_to(b_spatial[:, :, None], (SGU_HEADS, CHUNK, SGU_HEAD_DIM)),
        w_out.astype(bf16),
        final_g.reshape(1, D_MODEL),
    )
    return (_layer(x_prompt, weights, TILE), _layer(x_sample, weights, TILE))
```

```python
import functools

import jax
import jax.numpy as jnp
from jax import lax
from jax.experimental import pallas as pl
from jax.experimental.pallas import tpu as pltpu

D_MODEL = 1024
A_WIDTH = 512
B_WIDTH = 512
POOL_WINDOWS = (2, 4, 8, 16)
POOL_CH = 128
CHUNK = 128
SGU_HEADS = 4
SGU_HEAD_DIM = 128
IN_WIDTH = 2 * A_WIDTH + 3 * B_WIDTH
EPS = 1e-6

COL_A, COL_GATE_A, COL_U, COL_V, COL_GATE_B = 0, 512, 1024, 1536, 2048

HALO = max(POOL_WINDOWS) // 2
SUBLANES = 8
MXU_COLS = 256
TILE = 1024
W_OUT_PAD = 128
OUT_COLS = 512
OUT_ROWS = 256
VMEM_LIMIT_BYTES = 56 * 1024 * 1024


def _rmsnorm(x, g):
    ms = jnp.mean(x * x, axis=-1, keepdims=True)
    return x * lax.rsqrt(ms + EPS) * g


def _silu(x):
    hx = 0.5 * x
    return hx + hx * jnp.tanh(hx)


def _layer_kernel(seq_len, tile,
                  x_ref, xprev_ref, xnext_ref, norm_g_ref, w_in_ref, pool_w2_ref,
                  pool_scale_ref, ln_g_ref, ln_b_ref, w_s_ref, b_s_ref, w_out_ref,
                  final_g_ref, o_ref, a_scr, mix_scr):
    f32, bf16 = jnp.float32, jnp.bfloat16
    tiles_per_seq = seq_len // tile
    n_chunks = tile // CHUNK
    j = pl.program_id(0) % tiles_per_seq
    pos0 = j * tile

    def proj(hh, col0, width):
        return jnp.dot(hh, w_in_ref[:, col0:col0 + width], preferred_element_type=f32)

    norm_g = norm_g_ref[...]
    h = _rmsnorm(x_ref[...], norm_g).astype(bf16)

    x_halo = jnp.concatenate([xprev_ref[...], xnext_ref[...]], axis=0)
    a_halo = proj(_rmsnorm(x_halo, norm_g).astype(bf16), COL_A, A_WIDTH)
    a_scr[0:HALO, :] = jnp.where(j > 0, a_halo[0:HALO], 0.0)
    a_scr[HALO:HALO + tile, :] = proj(h, COL_A, A_WIDTH)
    a_scr[HALO + tile:, :] = jnp.where(j < tiles_per_seq - 1, a_halo[HALO:], 0.0)

    edge_rows = lax.broadcasted_iota(jnp.int32, (HALO, 1), 0)
    n_ext = tile + 2 * HALO
    sublane = lax.broadcasted_iota(jnp.int32, (1, SUBLANES, POOL_CH), 1)

    def shifted(x3, k):
        whole, part = divmod(k, SUBLANES)
        if whole:
            x3 = jnp.concatenate([x3[whole:], x3[:whole]], axis=0)
        if part:
            rot = pltpu.roll(x3, SUBLANES - part, 1)
            nxt = jnp.concatenate([rot[1:], rot[:1]], axis=0)
            x3 = jnp.where(sublane < SUBLANES - part, rot, nxt)
        return x3

    diffs = []
    for g, w in enumerate(POOL_WINDOWS):
        r = w // 2
        ext = a_scr[:, g * POOL_CH:(g + 1) * POOL_CH]
        run, span = ext.reshape(n_ext // SUBLANES, SUBLANES, POOL_CH), 1
        while span < r:
            run = run + shifted(run, span)
            span *= 2
        win = (run + shifted(run, n_ext - r)).reshape(n_ext, POOL_CH)

        def edge_diff(row0):
            t_abs = pos0 + row0 + edge_rows
            cnt = (jnp.minimum(t_abs + r, seq_len) - jnp.maximum(t_abs - r, 0)).astype(f32)
            rows = slice(HALO + row0, 2 * HALO + row0)
            return win[rows] / cnt - ext[rows]

        inner = slice(2 * HALO, tile)
        diff = jnp.concatenate([edge_diff(0), win[inner] * (1.0 / w) - ext[inner],
                                edge_diff(tile - HALO)], axis=0)
        diffs.append(diff.astype(bf16))

    gate_a = proj(h, COL_GATE_A, A_WIDTH)
    v = proj(h, COL_V, B_WIDTH)
    silu_a = _silu(gate_a)
    for pair in range(len(POOL_WINDOWS) // 2):
        cols = slice(pair * MXU_COLS, (pair + 1) * MXU_COLS)
        diff2 = jnp.concatenate(diffs[2 * pair:2 * pair + 2], axis=-1)
        y = jnp.dot(diff2, pool_w2_ref[pair], preferred_element_type=f32)
        mix_scr[:, cols] = (y * pool_scale_ref[:, cols] * silu_a[:, cols]).astype(bf16)

    mu = jnp.mean(v, axis=-1, keepdims=True)
    vc = v - mu
    var = jnp.mean(vc * vc, axis=-1, keepdims=True)
    vn = (vc * lax.rsqrt(var + EPS) * ln_g_ref[...] + ln_b_ref[...]).astype(bf16)
    gate_b = proj(h, COL_GATE_B, B_WIDTH)
    mixed_heads = []
    for hd in range(SGU_HEADS):
        hcols = slice(hd * SGU_HEAD_DIM, (hd + 1) * SGU_HEAD_DIM)
        vn_wide = jnp.concatenate(
            [vn[c * CHUNK:(c + 1) * CHUNK, hcols] for c in range(n_chunks)], axis=-1)
        mw = jnp.dot(w_s_ref[hd], vn_wide, preferred_element_type=f32)
        bias = b_s_ref[hd]
        mixed_heads.append(jnp.concatenate(
            [mw[:, c * SGU_HEAD_DIM:(c + 1) * SGU_HEAD_DIM] + bias for c in range(n_chunks)],
            axis=0))
    mixed = jnp.concatenate(mixed_heads, axis=-1)

    u = proj(h, COL_U, B_WIDTH)
    gated = mixed * _silu(gate_b)
    mix_scr[:, A_WIDTH:] = (u * gated).astype(bf16)

    final_g = final_g_ref[...]
    for rb in range(tile // OUT_ROWS):
        rows = slice(rb * OUT_ROWS, (rb + 1) * OUT_ROWS)
        mix = mix_scr[rows, :]
        delta = jnp.concatenate(
            [jnp.dot(mix, w_out_ref[:, c0:c0 + OUT_COLS], preferred_element_type=f32)
             for c0 in range(0, D_MODEL, OUT_COLS)], axis=-1)
        o_ref[rows, :] = _rmsnorm(x_ref[rows, :] + delta, final_g)


def _layer(x, weights, tile):
    bsz, seq_len, d = x.shape
    assert d == D_MODEL and seq_len % tile == 0 and tile % CHUNK == 0 and tile % OUT_ROWS == 0
    n_rows = bsz * seq_len
    x2 = x.reshape(n_rows, d)
    halo_blocks_per_tile = tile // HALO
    n_halo_blocks = n_rows // HALO

    def const(shape):
        return pl.BlockSpec(shape, lambda i: (0,) * len(shape))

    in_specs = [
        pl.BlockSpec((tile, d), lambda i: (i, 0)),
        pl.BlockSpec((HALO, d), lambda i: (jnp.maximum(i * halo_blocks_per_tile - 1, 0), 0)),
        pl.BlockSpec((HALO, d), lambda i: (jnp.minimum((i + 1) * halo_blocks_per_tile,
                                                       n_halo_blocks - 1), 0)),
        const((1, D_MODEL)),
        const((D_MODEL, IN_WIDTH)),
        const((len(POOL_WINDOWS) // 2, MXU_COLS, MXU_COLS)),
        const((1, A_WIDTH)),
        const((1, B_WIDTH)),
        const((1, B_WIDTH)),
        const((SGU_HEADS, CHUNK, CHUNK)),
        const((SGU_HEADS, CHUNK, SGU_HEAD_DIM)),
        const((D_MODEL, D_MODEL + W_OUT_PAD)),
        const((1, D_MODEL)),
    ]
    out = pl.pallas_call(
        functools.partial(_layer_kernel, seq_len, tile),
        grid=(n_rows // tile,),
        in_specs=in_specs,
        out_specs=pl.BlockSpec((tile, d), lambda i: (i, 0)),
        out_shape=jax.ShapeDtypeStruct((n_rows, d), x.dtype),
        scratch_shapes=[pltpu.VMEM((tile + 2 * HALO, A_WIDTH), jnp.float32),
                        pltpu.VMEM((tile, D_MODEL), jnp.bfloat16)],
        compiler_params=pltpu.CompilerParams(
            dimension_semantics=("arbitrary",),
            vmem_limit_bytes=VMEM_LIMIT_BYTES),
    )(x2, x2, x2, *weights)
    return out.reshape(bsz, seq_len, d)


def _pair_block_diagonal(pool_w):
    g, c, _ = pool_w.shape
    z = jnp.zeros((g // 2, c, c), pool_w.dtype)
    top = jnp.concatenate([pool_w[0::2], z], axis=-1)
    bot = jnp.concatenate([z, pool_w[1::2]], axis=-1)
    return jnp.concatenate([top, bot], axis=-2)


def kernel(x_prompt, x_sample, norm_g, w_in, pool_w, pool_scale, sgu_ln_g, sgu_ln_b,
           w_spatial, b_spatial, w_out, final_g):
    bf16 = jnp.bfloat16
    weights = (
        norm_g.reshape(1, D_MODEL),
        w_in.astype(bf16),
        _pair_block_diagonal(pool_w.astype(bf16)),
        pool_scale.reshape(1, A_WIDTH),
        sgu_ln_g.reshape(1, B_WIDTH),
        sgu_ln_b.reshape(1, B_WIDTH),
        w_spatial.astype(bf16),
        jnp.broadcast_to(b_spatial[:, :, None], (SGU_HEADS, CHUNK, SGU_HEAD_DIM)),
        jnp.pad(w_out.astype(bf16), ((0, 0), (0, W_OUT_PAD))),
        final_g.reshape(1, D_MODEL),
    )
    return (_layer(x_prompt, weights, TILE), _layer(x_sample, weights, TILE))
```

```python
import functools

import jax
import jax.numpy as jnp
from jax import lax
from jax.experimental import pallas as pl
from jax.experimental.pallas import tpu as pltpu

D_MODEL = 1024
A_WIDTH = 512
B_WIDTH = 512
POOL_WINDOWS = (2, 4, 8, 16)
POOL_CH = 128
CHUNK = 128
SGU_HEADS = 4
SGU_HEAD_DIM = 128
IN_WIDTH = 2 * A_WIDTH + 3 * B_WIDTH
EPS = 1e-6

COL_A, COL_GATE_A, COL_U, COL_V, COL_GATE_B = 0, 512, 1024, 1536, 2048

HALO = max(POOL_WINDOWS) // 2
SUBLANES = 8
TILE = 1024
W_OUT_PAD = 128
OUT_COLS = 512
OUT_ROWS = 256
PREP_ROWS = 256
VMEM_LIMIT_BYTES = 56 * 1024 * 1024


def _normalize(x):
    ms = jnp.mean(x * x, axis=-1, keepdims=True)
    return x * lax.rsqrt(ms + EPS)


def _silu(x):
    hx = 0.5 * x
    return hx + hx * jnp.tanh(hx)


def _prep_kernel(norm_g_ref, w_in_ref, pool_w_ref, pool_scale_ref, w_spatial_ref, b_spatial_ref,
                 w_out_ref, w_in_o, w_s_o, b_s_o, w_out_o):
    bf16 = jnp.bfloat16

    def as_column(row):
        n = row.shape[-1]
        on_diag = (lax.broadcasted_iota(jnp.int32, (n, n), 0)
                   == lax.broadcasted_iota(jnp.int32, (n, n), 1))
        return jnp.sum(jnp.where(on_diag, row, 0.0), axis=-1, keepdims=True)

    g = as_column(norm_g_ref[...])
    for grp in range(len(POOL_WINDOWS)):
        cols = slice(grp * POOL_CH, (grp + 1) * POOL_CH)
        folded = jnp.dot(w_in_ref[:, cols] * g, pool_w_ref[grp],
                         preferred_element_type=jnp.float32, precision=lax.Precision.HIGHEST)
        w_in_o[:, cols] = (folded * pool_scale_ref[:, cols]).astype(bf16)
    w_in_o[:, A_WIDTH:] = (w_in_ref[:, A_WIDTH:] * g).astype(bf16)
    w_out_o[:, :D_MODEL] = w_out_ref[...].astype(bf16)
    w_out_o[:, D_MODEL:] = jnp.zeros((w_out_o.shape[0], W_OUT_PAD), bf16)
    w_s_o[...] = w_spatial_ref[...].astype(bf16)
    for hd in range(SGU_HEADS):
        b_s_o[hd] = jnp.broadcast_to(as_column(b_spatial_ref[hd:hd + 1, :]), b_s_o.shape[1:])


def _prepare_weights(norm_g, w_in, pool_w, pool_scale, w_spatial, b_spatial, w_out):
    assert D_MODEL % PREP_ROWS == 0

    def const(shape):
        return pl.BlockSpec(shape, lambda i: (0,) * len(shape))

    def rows(width):
        return pl.BlockSpec((PREP_ROWS, width), lambda i: (i, 0))

    return pl.pallas_call(
        _prep_kernel,
        grid=(D_MODEL // PREP_ROWS,),
        in_specs=[pl.BlockSpec((1, PREP_ROWS), lambda i: (0, i)), rows(IN_WIDTH),
                  const(pool_w.shape), const((1, A_WIDTH)),
                  const(w_spatial.shape), const((SGU_HEADS, CHUNK)), rows(D_MODEL)],
        out_specs=[rows(IN_WIDTH), const(w_spatial.shape),
                   const((SGU_HEADS, CHUNK, SGU_HEAD_DIM)), rows(D_MODEL + W_OUT_PAD)],
        out_shape=[jax.ShapeDtypeStruct((D_MODEL, IN_WIDTH), jnp.bfloat16),
                   jax.ShapeDtypeStruct(w_spatial.shape, jnp.bfloat16),
                   jax.ShapeDtypeStruct((SGU_HEADS, CHUNK, SGU_HEAD_DIM), jnp.float32),
                   jax.ShapeDtypeStruct((D_MODEL, D_MODEL + W_OUT_PAD), jnp.bfloat16)],
        compiler_params=pltpu.CompilerParams(dimension_semantics=("arbitrary",)),
    )(norm_g.reshape(1, D_MODEL), w_in, pool_w, pool_scale.reshape(1, A_WIDTH),
      w_spatial, b_spatial, w_out)


def _layer_kernel(seq_len, tile,
                  x_ref, xprev_ref, xnext_ref, w_in_ref, ln_g_ref, ln_b_ref, w_s_ref, b_s_ref,
                  w_out_ref, final_g_ref, o_ref, a_scr, mix_scr):
    f32, bf16 = jnp.float32, jnp.bfloat16
    tiles_per_seq = seq_len // tile
    n_chunks = tile // CHUNK
    j = pl.program_id(0) % tiles_per_seq
    pos0 = j * tile

    def proj(hh, col0, width):
        return jnp.dot(hh, w_in_ref[:, col0:col0 + width], preferred_element_type=f32)

    h = _normalize(x_ref[...]).astype(bf16)

    x_halo = jnp.concatenate([xprev_ref[...], xnext_ref[...]], axis=0)
    a_halo = proj(_normalize(x_halo).astype(bf16), COL_A, A_WIDTH)
    a_scr[0:HALO, :] = jnp.where(j > 0, a_halo[0:HALO], 0.0)
    a_scr[HALO:HALO + tile, :] = proj(h, COL_A, A_WIDTH)
    a_scr[HALO + tile:, :] = jnp.where(j < tiles_per_seq - 1, a_halo[HALO:], 0.0)

    edge_rows = lax.broadcasted_iota(jnp.int32, (HALO, 1), 0)
    n_ext = tile + 2 * HALO
    sublane = lax.broadcasted_iota(jnp.int32, (1, SUBLANES, POOL_CH), 1)

    def shifted(x3, k):
        whole, part = divmod(k, SUBLANES)
        if whole:
            x3 = jnp.concatenate([x3[whole:], x3[:whole]], axis=0)
        if part:
            rot = pltpu.roll(x3, SUBLANES - part, 1)
            nxt = jnp.concatenate([rot[1:], rot[:1]], axis=0)
            x3 = jnp.where(sublane < SUBLANES - part, rot, nxt)
        return x3

    diffs = []
    for g, w in enumerate(POOL_WINDOWS):
        r = w // 2
        ext = a_scr[:, g * POOL_CH:(g + 1) * POOL_CH]
        run, span = ext.reshape(n_ext // SUBLANES, SUBLANES, POOL_CH), 1
        while span < r:
            run = run + shifted(run, span)
            span *= 2
        win = (run + shifted(run, n_ext - r)).reshape(n_ext, POOL_CH)

        def edge_diff(row0):
            t_abs = pos0 + row0 + edge_rows
            cnt = (jnp.minimum(t_abs + r, seq_len) - jnp.maximum(t_abs - r, 0)).astype(f32)
            rows = slice(HALO + row0, 2 * HALO + row0)
            return win[rows] / cnt - ext[rows]

        inner = slice(2 * HALO, tile)
        diffs.append(jnp.concatenate([edge_diff(0), win[inner] * (1.0 / w) - ext[inner],
                                      edge_diff(tile - HALO)], axis=0))

    gate_a = proj(h, COL_GATE_A, A_WIDTH)
    v = proj(h, COL_V, B_WIDTH)
    silu_a = _silu(gate_a)
    for g in range(len(POOL_WINDOWS)):
        cols = slice(g * POOL_CH, (g + 1) * POOL_CH)
        mix_scr[:, cols] = (diffs[g] * silu_a[:, cols]).astype(bf16)

    mu = jnp.mean(v, axis=-1, keepdims=True)
    vc = v - mu
    var = jnp.mean(vc * vc, axis=-1, keepdims=True)
    vn = (vc * lax.rsqrt(var + EPS) * ln_g_ref[...] + ln_b_ref[...]).astype(bf16)
    gate_b = proj(h, COL_GATE_B, B_WIDTH)
    mixed_heads = []
    for hd in range(SGU_HEADS):
        hcols = slice(hd * SGU_HEAD_DIM, (hd + 1) * SGU_HEAD_DIM)
        vn_wide = jnp.concatenate(
            [vn[c * CHUNK:(c + 1) * CHUNK, hcols] for c in range(n_chunks)], axis=-1)
        mw = jnp.dot(w_s_ref[hd], vn_wide, preferred_element_type=f32)
        bias = b_s_ref[hd]
        mixed_heads.append(jnp.concatenate(
            [mw[:, c * SGU_HEAD_DIM:(c + 1) * SGU_HEAD_DIM] + bias for c in range(n_chunks)],
            axis=0))
    mixed = jnp.concatenate(mixed_heads, axis=-1)

    u = proj(h, COL_U, B_WIDTH)
    gated = mixed * _silu(gate_b)
    mix_scr[:, A_WIDTH:] = (u * gated).astype(bf16)

    final_g = final_g_ref[...]
    for rb in range(tile // OUT_ROWS):
        rows = slice(rb * OUT_ROWS, (rb + 1) * OUT_ROWS)
        mix = mix_scr[rows, :]
        delta = jnp.concatenate(
            [jnp.dot(mix, w_out_ref[:, c0:c0 + OUT_COLS], preferred_element_type=f32)
             for c0 in range(0, D_MODEL, OUT_COLS)], axis=-1)
        o_ref[rows, :] = _normalize(x_ref[rows, :] + delta) * final_g


def _layer(x, weights, tile):
    bsz, seq_len, d = x.shape
    assert d == D_MODEL and seq_len % tile == 0 and tile % CHUNK == 0 and tile % OUT_ROWS == 0
    n_rows = bsz * seq_len
    x2 = x.reshape(n_rows, d)
    halo_blocks_per_tile = tile // HALO
    n_halo_blocks = n_rows // HALO

    def const(shape):
        return pl.BlockSpec(shape, lambda i: (0,) * len(shape))

    in_specs = [
        pl.BlockSpec((tile, d), lambda i: (i, 0)),
        pl.BlockSpec((HALO, d), lambda i: (jnp.maximum(i * halo_blocks_per_tile - 1, 0), 0)),
        pl.BlockSpec((HALO, d), lambda i: (jnp.minimum((i + 1) * halo_blocks_per_tile,
                                                       n_halo_blocks - 1), 0)),
        const((D_MODEL, IN_WIDTH)),
        const((1, B_WIDTH)),
        const((1, B_WIDTH)),
        const((SGU_HEADS, CHUNK, CHUNK)),
        const((SGU_HEADS, CHUNK, SGU_HEAD_DIM)),
        const((D_MODEL, D_MODEL + W_OUT_PAD)),
        const((1, D_MODEL)),
    ]
    out = pl.pallas_call(
        functools.partial(_layer_kernel, seq_len, tile),
        grid=(n_rows // tile,),
        in_specs=in_specs,
        out_specs=pl.BlockSpec((tile, d), lambda i: (i, 0)),
        out_shape=jax.ShapeDtypeStruct((n_rows, d), x.dtype),
        scratch_shapes=[pltpu.VMEM((tile + 2 * HALO, A_WIDTH), jnp.float32),
                        pltpu.VMEM((tile, D_MODEL), jnp.bfloat16)],
        compiler_params=pltpu.CompilerParams(
            dimension_semantics=("arbitrary",),
            vmem_limit_bytes=VMEM_LIMIT_BYTES),
    )(x2, x2, x2, *weights)
    return out.reshape(bsz, seq_len, d)


def kernel(x_prompt, x_sample, norm_g, w_in, pool_w, pool_scale, sgu_ln_g, sgu_ln_b,
           w_spatial, b_spatial, w_out, final_g):
    w_in_b, w_s_b, b_s_wide, w_out_b = _prepare_weights(
        norm_g, w_in, pool_w, pool_scale, w_spatial, b_spatial, w_out)
    weights = (w_in_b, sgu_ln_g.reshape(1, B_WIDTH), sgu_ln_b.reshape(1, B_WIDTH),
               w_s_b, b_s_wide, w_out_b, final_g.reshape(1, D_MODEL))
    return (_layer(x_prompt, weights, TILE), _layer(x_sample, weights, TILE))
```

```python
import functools

import jax
import jax.numpy as jnp
from jax import lax
from jax.experimental import pallas as pl
from jax.experimental.pallas import tpu as pltpu

D_MODEL = 1024
A_WIDTH = 512
B_WIDTH = 512
POOL_WINDOWS = (2, 4, 8, 16)
POOL_CH = 128
CHUNK = 128
SGU_HEADS = 4
SGU_HEAD_DIM = 128
IN_WIDTH = 2 * A_WIDTH + 3 * B_WIDTH
EPS = 1e-6

COL_A, COL_GATE_A, COL_U, COL_V, COL_GATE_B = 0, 512, 1024, 1536, 2048

HALO = max(POOL_WINDOWS) // 2
SUBLANES = 8
TILE = 1024
W_OUT_PAD = 128
OUT_COLS = 512
OUT_ROWS = 256
PREP_ROWS = 512
VMEM_LIMIT_BYTES = 56 * 1024 * 1024


def _normalize(x):
    ms = jnp.mean(x * x, axis=-1, keepdims=True)
    return x * lax.rsqrt(ms + EPS)


def _silu(x):
    hx = 0.5 * x
    return hx + hx * jnp.tanh(hx)


def _prep_kernel(norm_g_ref, w_in_ref, pool_w_ref, pool_scale_ref, w_spatial_ref, b_spatial_ref,
                 w_out_ref, w_in_o, w_s_o, b_s_o, w_out_o):
    bf16 = jnp.bfloat16

    def as_column(row):
        n = row.shape[-1]
        on_diag = (lax.broadcasted_iota(jnp.int32, (n, n), 0)
                   == lax.broadcasted_iota(jnp.int32, (n, n), 1))
        return jnp.sum(jnp.where(on_diag, row, 0.0), axis=-1, keepdims=True)

    g = as_column(norm_g_ref[...])
    for grp in range(len(POOL_WINDOWS)):
        cols = slice(grp * POOL_CH, (grp + 1) * POOL_CH)
        folded = jnp.dot(w_in_ref[:, cols] * g, pool_w_ref[grp],
                         preferred_element_type=jnp.float32, precision=lax.Precision.HIGHEST)
        w_in_o[:, cols] = (folded * pool_scale_ref[:, cols]).astype(bf16)
    w_in_o[:, A_WIDTH:] = (w_in_ref[:, A_WIDTH:] * g).astype(bf16)
    w_out_o[:, :D_MODEL] = w_out_ref[...].astype(bf16)
    w_out_o[:, D_MODEL:] = jnp.zeros((w_out_o.shape[0], W_OUT_PAD), bf16)
    w_s_o[...] = w_spatial_ref[...].astype(bf16)
    for hd in range(SGU_HEADS):
        b_s_o[hd] = jnp.broadcast_to(as_column(b_spatial_ref[hd:hd + 1, :]), b_s_o.shape[1:])


def _prepare_weights(norm_g, w_in, pool_w, pool_scale, w_spatial, b_spatial, w_out):
    assert D_MODEL % PREP_ROWS == 0

    def const(shape):
        return pl.BlockSpec(shape, lambda i: (0,) * len(shape))

    def rows(width):
        return pl.BlockSpec((PREP_ROWS, width), lambda i: (i, 0))

    return pl.pallas_call(
        _prep_kernel,
        grid=(D_MODEL // PREP_ROWS,),
        in_specs=[pl.BlockSpec((1, PREP_ROWS), lambda i: (0, i)), rows(IN_WIDTH),
                  const(pool_w.shape), const((1, A_WIDTH)),
                  const(w_spatial.shape), const((SGU_HEADS, CHUNK)), rows(D_MODEL)],
        out_specs=[rows(IN_WIDTH), const(w_spatial.shape),
                   const((SGU_HEADS, CHUNK, SGU_HEAD_DIM)), rows(D_MODEL + W_OUT_PAD)],
        out_shape=[jax.ShapeDtypeStruct((D_MODEL, IN_WIDTH), jnp.bfloat16),
                   jax.ShapeDtypeStruct(w_spatial.shape, jnp.bfloat16),
                   jax.ShapeDtypeStruct((SGU_HEADS, CHUNK, SGU_HEAD_DIM), jnp.float32),
                   jax.ShapeDtypeStruct((D_MODEL, D_MODEL + W_OUT_PAD), jnp.bfloat16)],
        compiler_params=pltpu.CompilerParams(dimension_semantics=("arbitrary",)),
    )(norm_g.reshape(1, D_MODEL), w_in, pool_w, pool_scale.reshape(1, A_WIDTH),
      w_spatial, b_spatial, w_out)


def _layer_kernel(seq_len, tile,
                  x_ref, xprev_ref, xnext_ref, w_in_ref, ln_g_ref, ln_b_ref, w_s_ref, b_s_ref,
                  w_out_ref, final_g_ref, o_ref):
    f32, bf16 = jnp.float32, jnp.bfloat16
    tiles_per_seq = seq_len // tile
    n_chunks = tile // CHUNK
    j = pl.program_id(0) % tiles_per_seq
    pos0 = j * tile

    def proj(hh, col0, width):
        return jnp.dot(hh, w_in_ref[:, col0:col0 + width], preferred_element_type=f32)

    h = _normalize(x_ref[...]).astype(bf16)

    x_halo = jnp.concatenate([xprev_ref[...], xnext_ref[...]], axis=0)
    a_halo = proj(_normalize(x_halo).astype(bf16), COL_A, A_WIDTH)
    a_ext = jnp.concatenate([jnp.where(j > 0, a_halo[0:HALO], 0.0),
                             proj(h, COL_A, A_WIDTH),
                             jnp.where(j < tiles_per_seq - 1, a_halo[HALO:], 0.0)], axis=0)

    edge_rows = lax.broadcasted_iota(jnp.int32, (HALO, 1), 0)
    n_ext = tile + 2 * HALO
    sublane = lax.broadcasted_iota(jnp.int32, (1, SUBLANES, POOL_CH), 1)

    def shifted(x3, k):
        whole, part = divmod(k, SUBLANES)
        if whole:
            x3 = jnp.concatenate([x3[whole:], x3[:whole]], axis=0)
        if part:
            rot = pltpu.roll(x3, SUBLANES - part, 1)
            nxt = jnp.concatenate([rot[1:], rot[:1]], axis=0)
            x3 = jnp.where(sublane < SUBLANES - part, rot, nxt)
        return x3

    diffs = []
    for g, w in enumerate(POOL_WINDOWS):
        r = w // 2
        ext = a_ext[:, g * POOL_CH:(g + 1) * POOL_CH]
        run, span = ext.reshape(n_ext // SUBLANES, SUBLANES, POOL_CH), 1
        while span < r:
            run = run + shifted(run, span)
            span *= 2
        win = (run + shifted(run, n_ext - r)).reshape(n_ext, POOL_CH)

        def edge_diff(row0):
            t_abs = pos0 + row0 + edge_rows
            cnt = (jnp.minimum(t_abs + r, seq_len) - jnp.maximum(t_abs - r, 0)).astype(f32)
            rows = slice(HALO + row0, 2 * HALO + row0)
            return win[rows] / cnt - ext[rows]

        inner = slice(2 * HALO, tile)
        diffs.append(jnp.concatenate([edge_diff(0), win[inner] * (1.0 / w) - ext[inner],
                                      edge_diff(tile - HALO)], axis=0))

    gate_a = proj(h, COL_GATE_A, A_WIDTH)
    v = proj(h, COL_V, B_WIDTH)
    silu_a = _silu(gate_a)
    mix_parts = [(diffs[g] * silu_a[:, g * POOL_CH:(g + 1) * POOL_CH]).astype(bf16)
                 for g in range(len(POOL_WINDOWS))]

    mu = jnp.mean(v, axis=-1, keepdims=True)
    vc = v - mu
    var = jnp.mean(vc * vc, axis=-1, keepdims=True)
    vn = (vc * lax.rsqrt(var + EPS) * ln_g_ref[...] + ln_b_ref[...]).astype(bf16)
    gate_b = proj(h, COL_GATE_B, B_WIDTH)
    mixed_heads = []
    for hd in range(SGU_HEADS):
        hcols = slice(hd * SGU_HEAD_DIM, (hd + 1) * SGU_HEAD_DIM)
        vn_wide = jnp.concatenate(
            [vn[c * CHUNK:(c + 1) * CHUNK, hcols] for c in range(n_chunks)], axis=-1)
        mw = jnp.dot(w_s_ref[hd], vn_wide, preferred_element_type=f32)
        bias = b_s_ref[hd]
        mixed_heads.append(jnp.concatenate(
            [mw[:, c * SGU_HEAD_DIM:(c + 1) * SGU_HEAD_DIM] + bias for c in range(n_chunks)],
            axis=0))
    mixed = jnp.concatenate(mixed_heads, axis=-1)

    u = proj(h, COL_U, B_WIDTH)
    gated = mixed * _silu(gate_b)
    mix_all = jnp.concatenate(mix_parts + [(u * gated).astype(bf16)], axis=-1)

    final_g = final_g_ref[...]
    for rb in range(tile // OUT_ROWS):
        rows = slice(rb * OUT_ROWS, (rb + 1) * OUT_ROWS)
        mix = mix_all[rows, :]
        delta = jnp.concatenate(
            [jnp.dot(mix, w_out_ref[:, c0:c0 + OUT_COLS], preferred_element_type=f32)
             for c0 in range(0, D_MODEL, OUT_COLS)], axis=-1)
        o_ref[rows, :] = _normalize(x_ref[rows, :] + delta) * final_g


def _layer(x, weights, tile):
    bsz, seq_len, d = x.shape
    assert d == D_MODEL and seq_len % tile == 0 and tile % CHUNK == 0 and tile % OUT_ROWS == 0
    n_rows = bsz * seq_len
    n_tiles = n_rows // tile
    x2 = x.reshape(n_rows, d)
    halo_blocks_per_tile = tile // HALO
    n_halo_blocks = n_rows // HALO

    def const(shape):
        return pl.BlockSpec(shape, lambda i: (0,) * len(shape))

    in_specs = [
        pl.BlockSpec((tile, d), lambda i: (i, 0)),
        pl.BlockSpec((HALO, d), lambda i: (jnp.maximum(i * halo_blocks_per_tile - 1, 0), 0)),
        pl.BlockSpec((HALO, d), lambda i: (jnp.minimum((i + 1) * halo_blocks_per_tile,
                                                       n_halo_blocks - 1), 0)),
        const((D_MODEL, IN_WIDTH)),
        const((1, B_WIDTH)),
        const((1, B_WIDTH)),
        const((SGU_HEADS, CHUNK, CHUNK)),
        const((SGU_HEADS, CHUNK, SGU_HEAD_DIM)),
        const((D_MODEL, D_MODEL + W_OUT_PAD)),
        const((1, D_MODEL)),
    ]
    out = pl.pallas_call(
        functools.partial(_layer_kernel, seq_len, tile),
        grid=(n_tiles,),
        in_specs=in_specs,
        out_specs=pl.BlockSpec((tile, d), lambda i: (i, 0)),
        out_shape=jax.ShapeDtypeStruct((n_rows, d), x.dtype),
        compiler_params=pltpu.CompilerParams(
            dimension_semantics=("arbitrary",),
            vmem_limit_bytes=VMEM_LIMIT_BYTES),
    )(x2, x2, x2, *weights)
    return out.reshape(bsz, seq_len, d)


def kernel(x_prompt, x_sample, norm_g, w_in, pool_w, pool_scale, sgu_ln_g, sgu_ln_b,
           w_spatial, b_spatial, w_out, final_g):
    w_in_b, w_s_b, b_s_wide, w_out_b = _prepare_weights(
        norm_g, w_in, pool_w, pool_scale, w_spatial, b_spatial, w_out)
    weights = (w_in_b, sgu_ln_g.reshape(1, B_WIDTH), sgu_ln_b.reshape(1, B_WIDTH),
               w_s_b, b_s_wide, w_out_b, final_g.reshape(1, D_MODEL))
    return (_layer(x_prompt, weights, TILE), _layer(x_sample, weights, TILE))
```

```python
import functools

import jax
import jax.numpy as jnp
from jax import lax
from jax.experimental import pallas as pl
from jax.experimental.pallas import tpu as pltpu

D_MODEL = 1024
A_WIDTH = 512
B_WIDTH = 512
POOL_WINDOWS = (2, 4, 8, 16)
POOL_CH = 128
CHUNK = 128
SGU_HEADS = 4
SGU_HEAD_DIM = 128
IN_WIDTH = 2 * A_WIDTH + 3 * B_WIDTH
EPS = 1e-6

COL_A, COL_GATE_A, COL_U, COL_V, COL_GATE_B = 0, 512, 1024, 1536, 2048

HALO = max(POOL_WINDOWS) // 2
SUBLANES = 8
TILE = 1024
W_OUT_PAD = 128
OUT_COLS = 512
OUT_ROWS = 256
PREP_ROWS = 512
VMEM_LIMIT_BYTES = 56 * 1024 * 1024


def _normalize(x):
    ms = jnp.mean(x * x, axis=-1, keepdims=True)
    return x * lax.rsqrt(ms + EPS)


def _silu(x):
    hx = 0.5 * x
    return hx + hx * jnp.tanh(hx)


def _prep_kernel(norm_g_ref, w_in_ref, pool_w_ref, pool_scale_ref, w_spatial_ref, b_spatial_ref,
                 w_out_ref, w_in_o, w_s_o, b_s_o, w_out_o):
    bf16 = jnp.bfloat16

    def as_column(row):
        n = row.shape[-1]
        on_diag = (lax.broadcasted_iota(jnp.int32, (n, n), 0)
                   == lax.broadcasted_iota(jnp.int32, (n, n), 1))
        return jnp.sum(jnp.where(on_diag, row, 0.0), axis=-1, keepdims=True)

    g = as_column(norm_g_ref[...])
    for grp in range(len(POOL_WINDOWS)):
        cols = slice(grp * POOL_CH, (grp + 1) * POOL_CH)
        folded = jnp.dot(w_in_ref[:, cols] * g, pool_w_ref[grp],
                         preferred_element_type=jnp.float32, precision=lax.Precision.HIGHEST)
        w_in_o[:, cols] = (folded * pool_scale_ref[:, cols]).astype(bf16)
    w_in_o[:, A_WIDTH:] = (w_in_ref[:, A_WIDTH:] * g).astype(bf16)
    w_out_o[:, :D_MODEL] = w_out_ref[...].astype(bf16)
    w_out_o[:, D_MODEL:] = jnp.zeros((w_out_o.shape[0], W_OUT_PAD), bf16)
    w_s_o[...] = w_spatial_ref[...].astype(bf16)
    for hd in range(SGU_HEADS):
        b_s_o[hd] = jnp.broadcast_to(as_column(b_spatial_ref[hd:hd + 1, :]), b_s_o.shape[1:])


def _prepare_weights(norm_g, w_in, pool_w, pool_scale, w_spatial, b_spatial, w_out):
    assert D_MODEL % PREP_ROWS == 0

    def const(shape):
        return pl.BlockSpec(shape, lambda i: (0,) * len(shape))

    def rows(width):
        return pl.BlockSpec((PREP_ROWS, width), lambda i: (i, 0))

    return pl.pallas_call(
        _prep_kernel,
        grid=(D_MODEL // PREP_ROWS,),
        in_specs=[pl.BlockSpec((1, PREP_ROWS), lambda i: (0, i)), rows(IN_WIDTH),
                  const(pool_w.shape), const((1, A_WIDTH)),
                  const(w_spatial.shape), const((SGU_HEADS, CHUNK)), rows(D_MODEL)],
        out_specs=[rows(IN_WIDTH), const(w_spatial.shape),
                   const((SGU_HEADS, CHUNK, SGU_HEAD_DIM)), rows(D_MODEL + W_OUT_PAD)],
        out_shape=[jax.ShapeDtypeStruct((D_MODEL, IN_WIDTH), jnp.bfloat16),
                   jax.ShapeDtypeStruct(w_spatial.shape, jnp.bfloat16),
                   jax.ShapeDtypeStruct((SGU_HEADS, CHUNK, SGU_HEAD_DIM), jnp.float32),
                   jax.ShapeDtypeStruct((D_MODEL, D_MODEL + W_OUT_PAD), jnp.bfloat16)],
        compiler_params=pltpu.CompilerParams(dimension_semantics=("arbitrary",)),
    )(norm_g.reshape(1, D_MODEL), w_in, pool_w, pool_scale.reshape(1, A_WIDTH),
      w_spatial, b_spatial, w_out)


def _tile_compute(seq_len, tiles_per_seq, j, tile,
                  x_ref, xprev_ref, xnext_ref, w_in_ref, ln_g_ref, ln_b_ref, w_s_ref, b_s_ref,
                  w_out_ref, final_g_ref, o_ref):
    f32, bf16 = jnp.float32, jnp.bfloat16
    n_chunks = tile // CHUNK
    pos0 = j * tile

    def proj(hh, col0, width):
        return jnp.dot(hh, w_in_ref[:, col0:col0 + width], preferred_element_type=f32)

    h = _normalize(x_ref[...]).astype(bf16)

    x_halo = jnp.concatenate([xprev_ref[...], xnext_ref[...]], axis=0)
    a_halo = proj(_normalize(x_halo).astype(bf16), COL_A, A_WIDTH)
    a_ext = jnp.concatenate([jnp.where(j > 0, a_halo[0:HALO], 0.0),
                             proj(h, COL_A, A_WIDTH),
                             jnp.where(j < tiles_per_seq - 1, a_halo[HALO:], 0.0)], axis=0)

    edge_rows = lax.broadcasted_iota(jnp.int32, (HALO, 1), 0)
    n_ext = tile + 2 * HALO
    sublane = lax.broadcasted_iota(jnp.int32, (1, SUBLANES, POOL_CH), 1)

    def shifted(x3, k):
        whole, part = divmod(k, SUBLANES)
        if whole:
            x3 = jnp.concatenate([x3[whole:], x3[:whole]], axis=0)
        if part:
            rot = pltpu.roll(x3, SUBLANES - part, 1)
            nxt = jnp.concatenate([rot[1:], rot[:1]], axis=0)
            x3 = jnp.where(sublane < SUBLANES - part, rot, nxt)
        return x3

    diffs = []
    for g, w in enumerate(POOL_WINDOWS):
        r = w // 2
        ext = a_ext[:, g * POOL_CH:(g + 1) * POOL_CH]
        run, span = ext.reshape(n_ext // SUBLANES, SUBLANES, POOL_CH), 1
        while span < r:
            run = run + shifted(run, span)
            span *= 2
        win = (run + shifted(run, n_ext - r)).reshape(n_ext, POOL_CH)

        def edge_diff(row0):
            t_abs = pos0 + row0 + edge_rows
            cnt = (jnp.minimum(t_abs + r, seq_len) - jnp.maximum(t_abs - r, 0)).astype(f32)
            rows = slice(HALO + row0, 2 * HALO + row0)
            return win[rows] / cnt - ext[rows]

        inner = slice(2 * HALO, tile)
        diffs.append(jnp.concatenate([edge_diff(0), win[inner] * (1.0 / w) - ext[inner],
                                      edge_diff(tile - HALO)], axis=0))

    gate_a = proj(h, COL_GATE_A, A_WIDTH)
    v = proj(h, COL_V, B_WIDTH)
    silu_a = _silu(gate_a)
    mix_parts = [(diffs[g] * silu_a[:, g * POOL_CH:(g + 1) * POOL_CH]).astype(bf16)
                 for g in range(len(POOL_WINDOWS))]

    mu = jnp.mean(v, axis=-1, keepdims=True)
    vc = v - mu
    var = jnp.mean(vc * vc, axis=-1, keepdims=True)
    vn = (vc * lax.rsqrt(var + EPS) * ln_g_ref[...] + ln_b_ref[...]).astype(bf16)
    gate_b = proj(h, COL_GATE_B, B_WIDTH)
    mixed_heads = []
    for hd in range(SGU_HEADS):
        hcols = slice(hd * SGU_HEAD_DIM, (hd + 1) * SGU_HEAD_DIM)
        vn_wide = jnp.concatenate(
            [vn[c * CHUNK:(c + 1) * CHUNK, hcols] for c in range(n_chunks)], axis=-1)
        mw = jnp.dot(w_s_ref[hd], vn_wide, preferred_element_type=f32)
        bias = b_s_ref[hd]
        mixed_heads.append(jnp.concatenate(
            [mw[:, c * SGU_HEAD_DIM:(c + 1) * SGU_HEAD_DIM] + bias for c in range(n_chunks)],
            axis=0))
    mixed = jnp.concatenate(mixed_heads, axis=-1)

    u = proj(h, COL_U, B_WIDTH)
    gated = mixed * _silu(gate_b)
    mix_all = jnp.concatenate(mix_parts + [(u * gated).astype(bf16)], axis=-1)

    final_g = final_g_ref[...]
    for rb in range(tile // OUT_ROWS):
        rows = slice(rb * OUT_ROWS, (rb + 1) * OUT_ROWS)
        mix = mix_all[rows, :]
        delta = jnp.concatenate(
            [jnp.dot(mix, w_out_ref[:, c0:c0 + OUT_COLS], preferred_element_type=f32)
             for c0 in range(0, D_MODEL, OUT_COLS)], axis=-1)
        o_ref[rows, :] = _normalize(x_ref[rows, :] + delta) * final_g


def _layers_kernel(tile, streams,
                   xa_hbm, xb_hbm, w_in_ref, ln_g_ref, ln_b_ref, w_s_ref, b_s_ref, w_out_ref,
                   final_g_ref, oa_hbm, ob_hbm, x_buf, halo_buf, o_buf, x_sem, halo_sem, o_sem):
    (n_a, seq_a), (n_b, seq_b) = streams
    n_steps = n_a + n_b
    i = pl.program_id(0)
    slot = i % 2

    def in_copies(hbm, n_tiles, t, to_slot):
        row0 = pl.multiple_of(t * tile, tile)
        before = pl.multiple_of(jnp.maximum(row0 - HALO, 0), HALO)
        after = pl.multiple_of(jnp.minimum(row0 + tile, n_tiles * tile - HALO), HALO)
        return (
            pltpu.make_async_copy(hbm.at[pl.ds(row0, tile), :], x_buf.at[to_slot],
                                  x_sem.at[to_slot]),
            pltpu.make_async_copy(hbm.at[pl.ds(before, HALO), :],
                                  halo_buf.at[to_slot, 0], halo_sem.at[to_slot, 0]),
            pltpu.make_async_copy(hbm.at[pl.ds(after, HALO), :],
                                  halo_buf.at[to_slot, 1], halo_sem.at[to_slot, 1]),
        )

    def out_copy(hbm, t, from_slot):
        return pltpu.make_async_copy(o_buf.at[from_slot],
                                     hbm.at[pl.ds(pl.multiple_of(t * tile, tile), tile), :],
                                     o_sem.at[from_slot])

    def start_fetch(step, to_slot):
        @pl.when(step < n_a)
        def _():
            for c in in_copies(xa_hbm, n_a, step, to_slot):
                c.start()

        @pl.when(step >= n_a)
        def _():
            for c in in_copies(xb_hbm, n_b, step - n_a, to_slot):
                c.start()

    @pl.when(i == 0)
    def _():
        start_fetch(i, slot)

    @pl.when(i >= 2)
    def _():
        out_copy(oa_hbm, 0, slot).wait()

    @pl.when(i + 1 < n_steps)
    def _():
        start_fetch(i + 1, 1 - slot)

    for c in in_copies(xa_hbm, n_a, 0, slot):
        c.wait()

    in_a = i < n_a
    t = jnp.where(in_a, i, i - n_a)
    seq_len = jnp.where(in_a, seq_a, seq_b)
    tiles_per_seq = jnp.where(in_a, seq_a // tile, seq_b // tile)
    _tile_compute(seq_len, tiles_per_seq, lax.rem(t, tiles_per_seq), tile,
                  x_buf.at[slot], halo_buf.at[slot, 0], halo_buf.at[slot, 1],
                  w_in_ref, ln_g_ref, ln_b_ref, w_s_ref, b_s_ref, w_out_ref, final_g_ref,
                  o_buf.at[slot])

    @pl.when(in_a)
    def _():
        out_copy(oa_hbm, t, slot).start()

    @pl.when(jnp.logical_not(in_a))
    def _():
        out_copy(ob_hbm, t, slot).start()

    @pl.when(i == n_steps - 1)
    def _():
        out_copy(oa_hbm, 0, 1 - slot).wait()
        out_copy(oa_hbm, 0, slot).wait()


def _layers(xa, xb, weights, tile):
    d = D_MODEL

    def stream(x):
        bsz, seq_len, dd = x.shape
        assert dd == d and seq_len % tile == 0 and tile % CHUNK == 0 and tile % OUT_ROWS == 0
        n_tiles = bsz * seq_len // tile
        assert n_tiles * tile >= 2 * HALO
        return x.reshape(bsz * seq_len, d), (n_tiles, seq_len)

    xa2, info_a = stream(xa)
    xb2, info_b = stream(xb)
    assert info_a[0] + info_b[0] >= 2

    def const(shape):
        return pl.BlockSpec(shape, lambda i: (0,) * len(shape))

    anywhere = pl.BlockSpec(memory_space=pl.ANY)
    in_specs = [
        anywhere,
        anywhere,
        const((D_MODEL, IN_WIDTH)),
        const((1, B_WIDTH)),
        const((1, B_WIDTH)),
        const((SGU_HEADS, CHUNK, CHUNK)),
        const((SGU_HEADS, CHUNK, SGU_HEAD_DIM)),
        const((D_MODEL, D_MODEL + W_OUT_PAD)),
        const((1, D_MODEL)),
    ]
    oa, ob = pl.pallas_call(
        functools.partial(_layers_kernel, tile, (info_a, info_b)),
        grid=(info_a[0] + info_b[0],),
        in_specs=in_specs,
        out_specs=[anywhere, anywhere],
        out_shape=[jax.ShapeDtypeStruct(xa2.shape, xa.dtype),
                   jax.ShapeDtypeStruct(xb2.shape, xb.dtype)],
        scratch_shapes=[pltpu.VMEM((2, tile, d), jnp.float32),
                        pltpu.VMEM((2, 2, HALO, d), jnp.float32),
                        pltpu.VMEM((2, tile, d), jnp.float32),
                        pltpu.SemaphoreType.DMA((2,)),
                        pltpu.SemaphoreType.DMA((2, 2)),
                        pltpu.SemaphoreType.DMA((2,))],
        compiler_params=pltpu.CompilerParams(
            dimension_semantics=("arbitrary",),
            vmem_limit_bytes=VMEM_LIMIT_BYTES),
    )(xa2, xb2, *weights)
    return oa.reshape(xa.shape), ob.reshape(xb.shape)


def kernel(x_prompt, x_sample, norm_g, w_in, pool_w, pool_scale, sgu_ln_g, sgu_ln_b,
           w_spatial, b_spatial, w_out, final_g):
    w_in_b, w_s_b, b_s_wide, w_out_b = _prepare_weights(
        norm_g, w_in, pool_w, pool_scale, w_spatial, b_spatial, w_out)
    weights = (w_in_b, sgu_ln_g.reshape(1, B_WIDTH), sgu_ln_b.reshape(1, B_WIDTH),
               w_s_b, b_s_wide, w_out_b, final_g.reshape(1, D_MODEL))
    return _layers(x_prompt, x_sample, weights, TILE)
```

```python
import functools

import jax
import jax.numpy as jnp
from jax import lax
from jax.experimental import pallas as pl
from jax.experimental.pallas import tpu as pltpu

D_MODEL = 1024
A_WIDTH = 512
B_WIDTH = 512
POOL_WINDOWS = (2, 4, 8, 16)
POOL_CH = 128
CHUNK = 128
SGU_HEADS = 4
SGU_HEAD_DIM = 128
IN_WIDTH = 2 * A_WIDTH + 3 * B_WIDTH
EPS = 1e-6

COL_A, COL_GATE_A, COL_U, COL_V, COL_GATE_B = 0, 512, 1024, 1536, 2048

HALO = max(POOL_WINDOWS) // 2
SUBLANES = 8
TILE = 1024
W_OUT_PAD = 128
OUT_COLS = 512
OUT_ROWS = 256
PREP_ROWS = 256
VMEM_LIMIT_BYTES = 56 * 1024 * 1024


def _normalize(x):
    ms = jnp.mean(x * x, axis=-1, keepdims=True)
    return x * lax.rsqrt(ms + EPS)


def _silu(x):
    hx = 0.5 * x
    return hx + hx * jnp.tanh(hx)


def _prepare_weights(norm_g_ref, pool_w_ref, pool_scale_ref, w_spatial_ref, b_spatial_ref,
                     w_in_hbm, w_out_hbm, stage_in, stage_out, stage_sem,
                     w_in_o, w_s_o, b_s_o, w_out_o):
    bf16 = jnp.bfloat16
    n_chunks = D_MODEL // PREP_ROWS

    def as_column(row):
        n = row.shape[-1]
        on_diag = (lax.broadcasted_iota(jnp.int32, (n, n), 0)
                   == lax.broadcasted_iota(jnp.int32, (n, n), 1))
        return jnp.sum(jnp.where(on_diag, row, 0.0), axis=-1, keepdims=True)

    def chunk_copies(k, slot):
        rows = pl.ds(k * PREP_ROWS, PREP_ROWS)
        return (pltpu.make_async_copy(w_in_hbm.at[rows, :], stage_in.at[slot], stage_sem.at[slot, 0]),
                pltpu.make_async_copy(w_out_hbm.at[rows, :], stage_out.at[slot], stage_sem.at[slot, 1]))

    for c in chunk_copies(0, 0):
        c.start()
    for k in range(n_chunks):
        slot = k % 2
        if k + 1 < n_chunks:
            for c in chunk_copies(k + 1, 1 - slot):
                c.start()
        for c in chunk_copies(k, slot):
            c.wait()
        rows = slice(k * PREP_ROWS, (k + 1) * PREP_ROWS)
        w_in_blk, w_out_blk = stage_in.at[slot], stage_out.at[slot]
        g = as_column(norm_g_ref[:, rows])
        for grp in range(len(POOL_WINDOWS)):
            cols = slice(grp * POOL_CH, (grp + 1) * POOL_CH)
            folded = jnp.dot(w_in_blk[:, cols] * g, pool_w_ref[grp],
                             preferred_element_type=jnp.float32,
                             precision=lax.Precision.HIGHEST)
            w_in_o[rows, cols] = (folded * pool_scale_ref[:, cols]).astype(bf16)
        w_in_o[rows, A_WIDTH:] = (w_in_blk[:, A_WIDTH:] * g).astype(bf16)
        w_out_o[rows, :D_MODEL] = w_out_blk[...].astype(bf16)
        w_out_o[rows, D_MODEL:] = jnp.zeros((PREP_ROWS, W_OUT_PAD), bf16)
    w_s_o[...] = w_spatial_ref[...].astype(bf16)
    for hd in range(SGU_HEADS):
        b_s_o[hd] = jnp.broadcast_to(as_column(b_spatial_ref[hd:hd + 1, :]), b_s_o.shape[1:])


def _tile_compute(seq_len, tiles_per_seq, j, tile,
                  x_ref, xprev_ref, xnext_ref, w_in_ref, ln_g_ref, ln_b_ref, w_s_ref, b_s_ref,
                  w_out_ref, final_g_ref, o_ref):
    f32, bf16 = jnp.float32, jnp.bfloat16
    n_chunks = tile // CHUNK
    pos0 = j * tile

    def proj(hh, col0, width):
        return jnp.dot(hh, w_in_ref[:, col0:col0 + width], preferred_element_type=f32)

    h = _normalize(x_ref[...]).astype(bf16)

    x_halo = jnp.concatenate([xprev_ref[...], xnext_ref[...]], axis=0)
    a_halo = proj(_normalize(x_halo).astype(bf16), COL_A, A_WIDTH)
    a_ext = jnp.concatenate([jnp.where(j > 0, a_halo[0:HALO], 0.0),
                             proj(h, COL_A, A_WIDTH),
                             jnp.where(j < tiles_per_seq - 1, a_halo[HALO:], 0.0)], axis=0)

    edge_rows = lax.broadcasted_iota(jnp.int32, (HALO, 1), 0)
    n_ext = tile + 2 * HALO
    sublane = lax.broadcasted_iota(jnp.int32, (1, SUBLANES, POOL_CH), 1)

    def shifted(x3, k):
        whole, part = divmod(k, SUBLANES)
        if whole:
            x3 = jnp.concatenate([x3[whole:], x3[:whole]], axis=0)
        if part:
            rot = pltpu.roll(x3, SUBLANES - part, 1)
            nxt = jnp.concatenate([rot[1:], rot[:1]], axis=0)
            x3 = jnp.where(sublane < SUBLANES - part, rot, nxt)
        return x3

    diffs = []
    for g, w in enumerate(POOL_WINDOWS):
        r = w // 2
        ext = a_ext[:, g * POOL_CH:(g + 1) * POOL_CH]
        run, span = ext.reshape(n_ext // SUBLANES, SUBLANES, POOL_CH), 1
        while span < r:
            run = run + shifted(run, span)
            span *= 2
        win = (run + shifted(run, n_ext - r)).reshape(n_ext, POOL_CH)

        def edge_diff(row0):
            t_abs = pos0 + row0 + edge_rows
            cnt = (jnp.minimum(t_abs + r, seq_len) - jnp.maximum(t_abs - r, 0)).astype(f32)
            rows = slice(HALO + row0, 2 * HALO + row0)
            return win[rows] / cnt - ext[rows]

        inner = slice(2 * HALO, tile)
        diffs.append(jnp.concatenate([edge_diff(0), win[inner] * (1.0 / w) - ext[inner],
                                      edge_diff(tile - HALO)], axis=0))

    gate_a = proj(h, COL_GATE_A, A_WIDTH)
    v = proj(h, COL_V, B_WIDTH)
    silu_a = _silu(gate_a)
    mix_parts = [(diffs[g] * silu_a[:, g * POOL_CH:(g + 1) * POOL_CH]).astype(bf16)
                 for g in range(len(POOL_WINDOWS))]

    mu = jnp.mean(v, axis=-1, keepdims=True)
    vc = v - mu
    var = jnp.mean(vc * vc, axis=-1, keepdims=True)
    vn = (vc * lax.rsqrt(var + EPS) * ln_g_ref[...] + ln_b_ref[...]).astype(bf16)
    gate_b = proj(h, COL_GATE_B, B_WIDTH)
    mixed_heads = []
    for hd in range(SGU_HEADS):
        hcols = slice(hd * SGU_HEAD_DIM, (hd + 1) * SGU_HEAD_DIM)
        vn_wide = jnp.concatenate(
            [vn[c * CHUNK:(c + 1) * CHUNK, hcols] for c in range(n_chunks)], axis=-1)
        mw = jnp.dot(w_s_ref[hd], vn_wide, preferred_element_type=f32)
        bias = b_s_ref[hd]
        mixed_heads.append(jnp.concatenate(
            [mw[:, c * SGU_HEAD_DIM:(c + 1) * SGU_HEAD_DIM] + bias for c in range(n_chunks)],
            axis=0))
    mixed = jnp.concatenate(mixed_heads, axis=-1)

    u = proj(h, COL_U, B_WIDTH)
    gated = mixed * _silu(gate_b)
    mix_all = jnp.concatenate(mix_parts + [(u * gated).astype(bf16)], axis=-1)

    final_g = final_g_ref[...]
    for rb in range(tile // OUT_ROWS):
        rows = slice(rb * OUT_ROWS, (rb + 1) * OUT_ROWS)
        mix = mix_all[rows, :]
        delta = jnp.concatenate(
            [jnp.dot(mix, w_out_ref[:, c0:c0 + OUT_COLS], preferred_element_type=f32)
             for c0 in range(0, D_MODEL, OUT_COLS)], axis=-1)
        o_ref[rows, :] = _normalize(x_ref[rows, :] + delta) * final_g


def _layers_kernel(tile, streams,
                   xa_hbm, xb_hbm, w_in_hbm, w_out_hbm, norm_g_ref, pool_w_ref, pool_scale_ref,
                   w_spatial_ref, b_spatial_ref, ln_g_ref, ln_b_ref, final_g_ref,
                   oa_hbm, ob_hbm,
                   x_buf, halo_buf, o_buf, x_sem, halo_sem, o_sem,
                   w_in_ref, w_s_ref, b_s_ref, w_out_ref, stage_in, stage_out, stage_sem):
    (n_a, seq_a), (n_b, seq_b) = streams
    n_steps = n_a + n_b
    i = pl.program_id(0)
    slot = i % 2

    def in_copies(hbm, n_tiles, t, to_slot):
        row0 = pl.multiple_of(t * tile, tile)
        before = pl.multiple_of(jnp.maximum(row0 - HALO, 0), HALO)
        after = pl.multiple_of(jnp.minimum(row0 + tile, n_tiles * tile - HALO), HALO)
        return (
            pltpu.make_async_copy(hbm.at[pl.ds(row0, tile), :], x_buf.at[to_slot],
                                  x_sem.at[to_slot]),
            pltpu.make_async_copy(hbm.at[pl.ds(before, HALO), :],
                                  halo_buf.at[to_slot, 0], halo_sem.at[to_slot, 0]),
            pltpu.make_async_copy(hbm.at[pl.ds(after, HALO), :],
                                  halo_buf.at[to_slot, 1], halo_sem.at[to_slot, 1]),
        )

    def out_copy(hbm, t, from_slot):
        return pltpu.make_async_copy(o_buf.at[from_slot],
                                     hbm.at[pl.ds(pl.multiple_of(t * tile, tile), tile), :],
                                     o_sem.at[from_slot])

    def start_fetch(step, to_slot):
        @pl.when(step < n_a)
        def _():
            for c in in_copies(xa_hbm, n_a, step, to_slot):
                c.start()

        @pl.when(step >= n_a)
        def _():
            for c in in_copies(xb_hbm, n_b, step - n_a, to_slot):
                c.start()

    @pl.when(i == 0)
    def _():
        start_fetch(i, slot)
        _prepare_weights(norm_g_ref, pool_w_ref, pool_scale_ref, w_spatial_ref, b_spatial_ref,
                         w_in_hbm, w_out_hbm, stage_in, stage_out, stage_sem,
                         w_in_ref, w_s_ref, b_s_ref, w_out_ref)

    @pl.when(i >= 2)
    def _():
        out_copy(oa_hbm, 0, slot).wait()

    @pl.when(i + 1 < n_steps)
    def _():
        start_fetch(i + 1, 1 - slot)

    for c in in_copies(xa_hbm, n_a, 0, slot):
        c.wait()

    in_a = i < n_a
    t = jnp.where(in_a, i, i - n_a)
    seq_len = jnp.where(in_a, seq_a, seq_b)
    tiles_per_seq = jnp.where(in_a, seq_a // tile, seq_b // tile)
    _tile_compute(seq_len, tiles_per_seq, lax.rem(t, tiles_per_seq), tile,
                  x_buf.at[slot], halo_buf.at[slot, 0], halo_buf.at[slot, 1],
                  w_in_ref, ln_g_ref, ln_b_ref, w_s_ref, b_s_ref, w_out_ref, final_g_ref,
                  o_buf.at[slot])

    @pl.when(in_a)
    def _():
        out_copy(oa_hbm, t, slot).start()

    @pl.when(jnp.logical_not(in_a))
    def _():
        out_copy(ob_hbm, t, slot).start()

    @pl.when(i == n_steps - 1)
    def _():
        out_copy(oa_hbm, 0, 1 - slot).wait()
        out_copy(oa_hbm, 0, slot).wait()


def _layers(xa, xb, norm_g, w_in, pool_w, pool_scale, sgu_ln_g, sgu_ln_b, w_spatial, b_spatial,
            w_out, final_g, tile):
    d = D_MODEL
    assert D_MODEL % PREP_ROWS == 0

    def stream(x):
        bsz, seq_len, dd = x.shape
        assert dd == d and seq_len % tile == 0 and tile % CHUNK == 0 and tile % OUT_ROWS == 0
        n_tiles = bsz * seq_len // tile
        assert n_tiles * tile >= 2 * HALO
        return x.reshape(bsz * seq_len, d), (n_tiles, seq_len)

    xa2, info_a = stream(xa)
    xb2, info_b = stream(xb)
    assert info_a[0] + info_b[0] >= 2

    def const(shape):
        return pl.BlockSpec(shape, lambda i: (0,) * len(shape))

    anywhere = pl.BlockSpec(memory_space=pl.ANY)
    in_specs = [
        anywhere,
        anywhere,
        anywhere,
        anywhere,
        const((1, D_MODEL)),
        const(pool_w.shape),
        const((1, A_WIDTH)),
        const(w_spatial.shape),
        const(b_spatial.shape),
        const((1, B_WIDTH)),
        const((1, B_WIDTH)),
        const((1, D_MODEL)),
    ]
    oa, ob = pl.pallas_call(
        functools.partial(_layers_kernel, tile, (info_a, info_b)),
        grid=(info_a[0] + info_b[0],),
        in_specs=in_specs,
        out_specs=[anywhere, anywhere],
        out_shape=[jax.ShapeDtypeStruct(xa2.shape, xa.dtype),
                   jax.ShapeDtypeStruct(xb2.shape, xb.dtype)],
        scratch_shapes=[pltpu.VMEM((2, tile, d), jnp.float32),
                        pltpu.VMEM((2, 2, HALO, d), jnp.float32),
                        pltpu.VMEM((2, tile, d), jnp.float32),
                        pltpu.SemaphoreType.DMA((2,)),
                        pltpu.SemaphoreType.DMA((2, 2)),
                        pltpu.SemaphoreType.DMA((2,)),
                        pltpu.VMEM((D_MODEL, IN_WIDTH), jnp.bfloat16),
                        pltpu.VMEM(w_spatial.shape, jnp.bfloat16),
                        pltpu.VMEM((SGU_HEADS, CHUNK, SGU_HEAD_DIM), jnp.float32),
                        pltpu.VMEM((D_MODEL, D_MODEL + W_OUT_PAD), jnp.bfloat16),
                        pltpu.VMEM((2, PREP_ROWS, IN_WIDTH), jnp.float32),
                        pltpu.VMEM((2, PREP_ROWS, D_MODEL), jnp.float32),
                        pltpu.SemaphoreType.DMA((2, 2))],
        compiler_params=pltpu.CompilerParams(
            dimension_semantics=("arbitrary",),
            vmem_limit_bytes=VMEM_LIMIT_BYTES),
    )(xa2, xb2, w_in, w_out, norm_g.reshape(1, D_MODEL), pool_w, pool_scale.reshape(1, A_WIDTH),
      w_spatial, b_spatial, sgu_ln_g.reshape(1, B_WIDTH), sgu_ln_b.reshape(1, B_WIDTH),
      final_g.reshape(1, D_MODEL))
    return oa.reshape(xa.shape), ob.reshape(xb.shape)


def kernel(x_prompt, x_sample, norm_g, w_in, pool_w, pool_scale, sgu_ln_g, sgu_ln_b,
           w_spatial, b_spatial, w_out, final_g):
    return _layers(x_prompt, x_sample, norm_g, w_in, pool_w, pool_scale, sgu_ln_g, sgu_ln_b,
                   w_spatial, b_spatial, w_out, final_g, TILE)
```

```python
import functools

import jax
import jax.numpy as jnp
from jax import lax
from jax.experimental import pallas as pl
from jax.experimental.pallas import tpu as pltpu

D_MODEL = 1024
A_WIDTH = 512
B_WIDTH = 512
POOL_WINDOWS = (2, 4, 8, 16)
POOL_CH = 128
CHUNK = 128
SGU_HEADS = 4
SGU_HEAD_DIM = 128
IN_WIDTH = 2 * A_WIDTH + 3 * B_WIDTH
EPS = 1e-6

COL_A, COL_GATE_A, COL_U, COL_V, COL_GATE_B = 0, 512, 1024, 1536, 2048

HALO = max(POOL_WINDOWS) // 2
SUBLANES = 8
TILE = 1024
W_OUT_PAD = 128
OUT_COLS = 512
OUT_ROWS = 256
PREP_ROWS = 256
VMEM_LIMIT_BYTES = 56 * 1024 * 1024


def _normalize(x):
    ms = jnp.mean(x * x, axis=-1, keepdims=True)
    return x * lax.rsqrt(ms + EPS)


def _silu_of_half(hx):
    return hx + hx * jnp.tanh(hx)


def _prepare_weights(norm_g_ref, pool_w_ref, pool_scale_ref, w_spatial_ref, b_spatial_ref, ln_b_ref,
                     w_in_hbm, w_out_hbm, stage_in, stage_out, stage_sem,
                     w_in_o, w_s_o, b_s_o, w_out_o):
    bf16 = jnp.bfloat16
    n_chunks = D_MODEL // PREP_ROWS

    def as_column(row):
        n = row.shape[-1]
        on_diag = (lax.broadcasted_iota(jnp.int32, (n, n), 0)
                   == lax.broadcasted_iota(jnp.int32, (n, n), 1))
        return jnp.sum(jnp.where(on_diag, row, 0.0), axis=-1, keepdims=True)

    def chunk_copies(k, slot):
        rows = pl.ds(k * PREP_ROWS, PREP_ROWS)
        return (pltpu.make_async_copy(w_in_hbm.at[rows, :], stage_in.at[slot], stage_sem.at[slot, 0]),
                pltpu.make_async_copy(w_out_hbm.at[rows, :], stage_out.at[slot], stage_sem.at[slot, 1]))

    for c in chunk_copies(0, 0):
        c.start()
    for k in range(n_chunks):
        slot = k % 2
        if k + 1 < n_chunks:
            for c in chunk_copies(k + 1, 1 - slot):
                c.start()
        for c in chunk_copies(k, slot):
            c.wait()
        rows = slice(k * PREP_ROWS, (k + 1) * PREP_ROWS)
        w_in_blk, w_out_blk = stage_in.at[slot], stage_out.at[slot]
        g = as_column(norm_g_ref[:, rows])
        for grp in range(len(POOL_WINDOWS)):
            cols = slice(grp * POOL_CH, (grp + 1) * POOL_CH)
            folded = jnp.dot(w_in_blk[:, cols] * g, pool_w_ref[grp],
                             preferred_element_type=jnp.float32,
                             precision=lax.Precision.HIGHEST)
            w_in_o[rows, cols] = (folded * pool_scale_ref[:, cols]).astype(bf16)
        for col0, width, col_scale in ((COL_GATE_A, A_WIDTH, 0.5), (COL_U, B_WIDTH, 1.0),
                                       (COL_V, B_WIDTH, 1.0), (COL_GATE_B, B_WIDTH, 0.5)):
            cols = slice(col0, col0 + width)
            w_in_o[rows, cols] = (w_in_blk[:, cols] * (col_scale * g)).astype(bf16)
        w_out_o[rows, :D_MODEL] = w_out_blk[...].astype(bf16)
        w_out_o[rows, D_MODEL:] = jnp.zeros((PREP_ROWS, W_OUT_PAD), bf16)
    w_s_o[...] = w_spatial_ref[...].astype(bf16)
    for hd in range(SGU_HEADS):
        hcols = slice(hd * SGU_HEAD_DIM, (hd + 1) * SGU_HEAD_DIM)
        w_rowsum = jnp.sum(w_spatial_ref[hd], axis=-1, keepdims=True)
        b_s_o[hd] = w_rowsum * ln_b_ref[:, hcols] + as_column(b_spatial_ref[hd:hd + 1, :])


def _tile_compute(seq_len, tiles_per_seq, j, tile,
                  x_ref, xprev_ref, xnext_ref, w_in_ref, ln_g_ref, w_s_ref, b_s_ref,
                  w_out_ref, final_g_ref, o_ref):
    f32, bf16 = jnp.float32, jnp.bfloat16
    n_chunks = tile // CHUNK
    pos0 = j * tile

    def proj(hh, col0, width):
        return jnp.dot(hh, w_in_ref[:, col0:col0 + width], preferred_element_type=f32)

    h = _normalize(x_ref[...]).astype(bf16)

    x_halo = jnp.concatenate([xprev_ref[...], xnext_ref[...]], axis=0)
    a_halo = proj(_normalize(x_halo).astype(bf16), COL_A, A_WIDTH)
    a_ext = jnp.concatenate([jnp.where(j > 0, a_halo[0:HALO], 0.0),
                             proj(h, COL_A, A_WIDTH),
                             jnp.where(j < tiles_per_seq - 1, a_halo[HALO:], 0.0)], axis=0)

    edge_rows = lax.broadcasted_iota(jnp.int32, (HALO, 1), 0)
    n_ext = tile + 2 * HALO
    sublane = lax.broadcasted_iota(jnp.int32, (1, SUBLANES, POOL_CH), 1)

    def shifted(x3, k):
        whole, part = divmod(k, SUBLANES)
        if whole:
            x3 = jnp.concatenate([x3[whole:], x3[:whole]], axis=0)
        if part:
            rot = pltpu.roll(x3, SUBLANES - part, 1)
            nxt = jnp.concatenate([rot[1:], rot[:1]], axis=0)
            x3 = jnp.where(sublane < SUBLANES - part, rot, nxt)
        return x3

    diffs = []
    for g, w in enumerate(POOL_WINDOWS):
        r = w // 2
        ext = a_ext[:, g * POOL_CH:(g + 1) * POOL_CH]
        run, span = ext.reshape(n_ext // SUBLANES, SUBLANES, POOL_CH), 1
        while span < r:
            run = run + shifted(run, span)
            span *= 2
        win = (run + shifted(run, n_ext - r)).reshape(n_ext, POOL_CH)

        def edge_diff(row0):
            t_abs = pos0 + row0 + edge_rows
            cnt = (jnp.minimum(t_abs + r, seq_len) - jnp.maximum(t_abs - r, 0)).astype(f32)
            rows = slice(HALO + row0, 2 * HALO + row0)
            return win[rows] / cnt - ext[rows]

        inner = slice(2 * HALO, tile)
        diffs.append(jnp.concatenate([edge_diff(0), win[inner] * (1.0 / w) - ext[inner],
                                      edge_diff(tile - HALO)], axis=0))

    half_gate_a = proj(h, COL_GATE_A, A_WIDTH)
    v = proj(h, COL_V, B_WIDTH)
    silu_a = _silu_of_half(half_gate_a)
    mix_parts = [(diffs[g] * silu_a[:, g * POOL_CH:(g + 1) * POOL_CH]).astype(bf16)
                 for g in range(len(POOL_WINDOWS))]

    mu = jnp.mean(v, axis=-1, keepdims=True)
    vc = v - mu
    var = jnp.mean(vc * vc, axis=-1, keepdims=True)
    vn = (vc * lax.rsqrt(var + EPS)).astype(bf16)
    half_gate_b = proj(h, COL_GATE_B, B_WIDTH)
    mixed_heads = []
    for hd in range(SGU_HEADS):
        hcols = slice(hd * SGU_HEAD_DIM, (hd + 1) * SGU_HEAD_DIM)
        vn_wide = jnp.concatenate(
            [vn[c * CHUNK:(c + 1) * CHUNK, hcols] for c in range(n_chunks)], axis=-1)
        mw = jnp.dot(w_s_ref[hd], vn_wide, preferred_element_type=f32)
        gain, shift = ln_g_ref[:, hcols], b_s_ref[hd]
        mixed_heads.append(jnp.concatenate(
            [mw[:, c * SGU_HEAD_DIM:(c + 1) * SGU_HEAD_DIM] * gain + shift
             for c in range(n_chunks)], axis=0))
    mixed = jnp.concatenate(mixed_heads, axis=-1)

    u = proj(h, COL_U, B_WIDTH)
    gated = mixed * _silu_of_half(half_gate_b)
    mix_all = jnp.concatenate(mix_parts + [(u * gated).astype(bf16)], axis=-1)

    final_g = final_g_ref[...]
    for rb in range(tile // OUT_ROWS):
        rows = slice(rb * OUT_ROWS, (rb + 1) * OUT_ROWS)
        mix = mix_all[rows, :]
        delta = jnp.concatenate(
            [jnp.dot(mix, w_out_ref[:, c0:c0 + OUT_COLS], preferred_element_type=f32)
             for c0 in range(0, D_MODEL, OUT_COLS)], axis=-1)
        o_ref[rows, :] = _normalize(x_ref[rows, :] + delta) * final_g


def _layers_kernel(tile, streams,
                   xa_hbm, xb_hbm, w_in_hbm, w_out_hbm, norm_g_ref, pool_w_ref, pool_scale_ref,
                   w_spatial_ref, b_spatial_ref, ln_g_ref, ln_b_ref, final_g_ref,
                   oa_hbm, ob_hbm,
                   x_buf, halo_buf, o_buf, x_sem, halo_sem, o_sem,
                   w_in_ref, w_s_ref, b_s_ref, w_out_ref, stage_in, stage_out, stage_sem):
    (n_a, seq_a), (n_b, seq_b) = streams
    n_steps = n_a + n_b
    i = pl.program_id(0)
    slot = i % 2

    def in_copies(hbm, n_tiles, t, to_slot):
        row0 = pl.multiple_of(t * tile, tile)
        before = pl.multiple_of(jnp.maximum(row0 - HALO, 0), HALO)
        after = pl.multiple_of(jnp.minimum(row0 + tile, n_tiles * tile - HALO), HALO)
        return (
            pltpu.make_async_copy(hbm.at[pl.ds(row0, tile), :], x_buf.at[to_slot],
                                  x_sem.at[to_slot]),
            pltpu.make_async_copy(hbm.at[pl.ds(before, HALO), :],
                                  halo_buf.at[to_slot, 0], halo_sem.at[to_slot, 0]),
            pltpu.make_async_copy(hbm.at[pl.ds(after, HALO), :],
                                  halo_buf.at[to_slot, 1], halo_sem.at[to_slot, 1]),
        )

    def out_copy(hbm, t, from_slot):
        return pltpu.make_async_copy(o_buf.at[from_slot],
                                     hbm.at[pl.ds(pl.multiple_of(t * tile, tile), tile), :],
                                     o_sem.at[from_slot])

    def start_fetch(step, to_slot):
        @pl.when(step < n_a)
        def _():
            for c in in_copies(xa_hbm, n_a, step, to_slot):
                c.start()

        @pl.when(step >= n_a)
        def _():
            for c in in_copies(xb_hbm, n_b, step - n_a, to_slot):
                c.start()

    @pl.when(i == 0)
    def _():
        start_fetch(i, slot)
        _prepare_weights(norm_g_ref, pool_w_ref, pool_scale_ref, w_spatial_ref, b_spatial_ref,
                         ln_b_ref, w_in_hbm, w_out_hbm, stage_in, stage_out, stage_sem,
                         w_in_ref, w_s_ref, b_s_ref, w_out_ref)

    @pl.when(i >= 2)
    def _():
        out_copy(oa_hbm, 0, slot).wait()

    @pl.when(i + 1 < n_steps)
    def _():
        start_fetch(i + 1, 1 - slot)

    for c in in_copies(xa_hbm, n_a, 0, slot):
        c.wait()

    in_a = i < n_a
    t = jnp.where(in_a, i, i - n_a)
    seq_len = jnp.where(in_a, seq_a, seq_b)
    tiles_per_seq = jnp.where(in_a, seq_a // tile, seq_b // tile)
    _tile_compute(seq_len, tiles_per_seq, lax.rem(t, tiles_per_seq), tile,
                  x_buf.at[slot], halo_buf.at[slot, 0], halo_buf.at[slot, 1],
                  w_in_ref, ln_g_ref, w_s_ref, b_s_ref, w_out_ref, final_g_ref,
                  o_buf.at[slot])

    @pl.when(in_a)
    def _():
        out_copy(oa_hbm, t, slot).start()

    @pl.when(jnp.logical_not(in_a))
    def _():
        out_copy(ob_hbm, t, slot).start()

    @pl.when(i == n_steps - 1)
    def _():
        out_copy(oa_hbm, 0, 1 - slot).wait()
        out_copy(oa_hbm, 0, slot).wait()


def _layers(xa, xb, norm_g, w_in, pool_w, pool_scale, sgu_ln_g, sgu_ln_b, w_spatial, b_spatial,
            w_out, final_g, tile):
    d = D_MODEL
    assert D_MODEL % PREP_ROWS == 0

    def stream(x):
        bsz, seq_len, dd = x.shape
        assert dd == d and seq_len % tile == 0 and tile % CHUNK == 0 and tile % OUT_ROWS == 0
        n_tiles = bsz * seq_len // tile
        assert n_tiles * tile >= 2 * HALO
        return x.reshape(bsz * seq_len, d), (n_tiles, seq_len)

    xa2, info_a = stream(xa)
    xb2, info_b = stream(xb)
    assert info_a[0] + info_b[0] >= 2

    def const(shape):
        return pl.BlockSpec(shape, lambda i: (0,) * len(shape))

    anywhere = pl.BlockSpec(memory_space=pl.ANY)
    in_specs = [
        anywhere,
        anywhere,
        anywhere,
        anywhere,
        const((1, D_MODEL)),
        const(pool_w.shape),
        const((1, A_WIDTH)),
        const(w_spatial.shape),
        const(b_spatial.shape),
        const((1, B_WIDTH)),
        const((1, B_WIDTH)),
        const((1, D_MODEL)),
    ]
    oa, ob = pl.pallas_call(
        functools.partial(_layers_kernel, tile, (info_a, info_b)),
        grid=(info_a[0] + info_b[0],),
        in_specs=in_specs,
        out_specs=[anywhere, anywhere],
        out_shape=[jax.ShapeDtypeStruct(xa2.shape, xa.dtype),
                   jax.ShapeDtypeStruct(xb2.shape, xb.dtype)],
        scratch_shapes=[pltpu.VMEM((2, tile, d), jnp.float32),
                        pltpu.VMEM((2, 2, HALO, d), jnp.float32),
                        pltpu.VMEM((2, tile, d), jnp.float32),
                        pltpu.SemaphoreType.DMA((2,)),
                        pltpu.SemaphoreType.DMA((2, 2)),
                        pltpu.SemaphoreType.DMA((2,)),
                        pltpu.VMEM((D_MODEL, IN_WIDTH), jnp.bfloat16),
                        pltpu.VMEM(w_spatial.shape, jnp.bfloat16),
                        pltpu.VMEM((SGU_HEADS, CHUNK, SGU_HEAD_DIM), jnp.float32),
                        pltpu.VMEM((D_MODEL, D_MODEL + W_OUT_PAD), jnp.bfloat16),
                        pltpu.VMEM((2, PREP_ROWS, IN_WIDTH), jnp.float32),
                        pltpu.VMEM((2, PREP_ROWS, D_MODEL), jnp.float32),
                        pltpu.SemaphoreType.DMA((2, 2))],
        compiler_params=pltpu.CompilerParams(
            dimension_semantics=("arbitrary",),
            vmem_limit_bytes=VMEM_LIMIT_BYTES),
    )(xa2, xb2, w_in, w_out, norm_g.reshape(1, D_MODEL), pool_w, pool_scale.reshape(1, A_WIDTH),
      w_spatial, b_spatial, sgu_ln_g.reshape(1, B_WIDTH), sgu_ln_b.reshape(1, B_WIDTH),
      final_g.reshape(1, D_MODEL))
    return oa.reshape(xa.shape), ob.reshape(xb.shape)


def kernel(x_prompt, x_sample, norm_g, w_in, pool_w, pool_scale, sgu_ln_g, sgu_ln_b,
           w_spatial, b_spatial, w_out, final_g):
    return _layers(x_prompt, x_sample, norm_g, w_in, pool_w, pool_scale, sgu_ln_g, sgu_ln_b,
                   w_spatial, b_spatial, w_out, final_g, TILE)
```

```python
import functools

import jax
import jax.numpy as jnp
from jax import lax
from jax.experimental import pallas as pl
from jax.experimental.pallas import tpu as pltpu

D_MODEL = 1024
A_WIDTH = 512
B_WIDTH = 512
POOL_WINDOWS = (2, 4, 8, 16)
POOL_CH = 128
CHUNK = 128
SGU_HEADS = 4
SGU_HEAD_DIM = 128
IN_WIDTH = 2 * A_WIDTH + 3 * B_WIDTH
EPS = 1e-6

COL_A, COL_GATE_A, COL_U, COL_V, COL_GATE_B = 0, 512, 1024, 1536, 2048

HALO = max(POOL_WINDOWS) // 2
SUBLANES = 8
TILE = 1024
W_OUT_PAD = 128
OUT_COLS = 512
OUT_ROWS = 256
PREP_ROWS = 256
VMEM_LIMIT_BYTES = 56 * 1024 * 1024


def _normalize(x):
    ms = jnp.mean(x * x, axis=-1, keepdims=True)
    return x * lax.rsqrt(ms + EPS)


def _silu_of_half(hx):
    return hx + hx * jnp.tanh(hx)


def _prepare_weights(norm_g_ref, pool_w_ref, pool_scale_ref, w_spatial_ref, b_spatial_ref, ln_b_ref,
                     w_in_hbm, w_out_hbm, stage_in, stage_out, stage_sem,
                     w_in_o, w_s_o, b_s_o, w_out_o):
    bf16 = jnp.bfloat16
    n_chunks = D_MODEL // PREP_ROWS

    def as_column(row):
        n = row.shape[-1]
        on_diag = (lax.broadcasted_iota(jnp.int32, (n, n), 0)
                   == lax.broadcasted_iota(jnp.int32, (n, n), 1))
        return jnp.sum(jnp.where(on_diag, row, 0.0), axis=-1, keepdims=True)

    def chunk_copies(k, slot):
        rows = pl.ds(k * PREP_ROWS, PREP_ROWS)
        return (pltpu.make_async_copy(w_in_hbm.at[rows, :], stage_in.at[slot], stage_sem.at[slot, 0]),
                pltpu.make_async_copy(w_out_hbm.at[rows, :], stage_out.at[slot], stage_sem.at[slot, 1]))

    for c in chunk_copies(0, 0):
        c.start()
    for k in range(n_chunks):
        slot = k % 2
        if k + 1 < n_chunks:
            for c in chunk_copies(k + 1, 1 - slot):
                c.start()
        for c in chunk_copies(k, slot):
            c.wait()
        rows = slice(k * PREP_ROWS, (k + 1) * PREP_ROWS)
        w_in_blk, w_out_blk = stage_in.at[slot], stage_out.at[slot]
        g = as_column(norm_g_ref[:, rows])
        for grp in range(len(POOL_WINDOWS)):
            cols = slice(grp * POOL_CH, (grp + 1) * POOL_CH)
            folded = jnp.dot(w_in_blk[:, cols] * g, pool_w_ref[grp],
                             preferred_element_type=jnp.float32,
                             precision=lax.Precision.HIGHEST)
            w_in_o[rows, cols] = (folded * pool_scale_ref[:, cols]).astype(bf16)
        for col0, width, col_scale in ((COL_GATE_A, A_WIDTH, 0.5), (COL_U, B_WIDTH, 1.0),
                                       (COL_V, B_WIDTH, 1.0), (COL_GATE_B, B_WIDTH, 0.5)):
            cols = slice(col0, col0 + width)
            w_in_o[rows, cols] = (w_in_blk[:, cols] * (col_scale * g)).astype(bf16)
        w_out_o[rows, :D_MODEL] = w_out_blk[...].astype(bf16)
        w_out_o[rows, D_MODEL:] = jnp.zeros((PREP_ROWS, W_OUT_PAD), bf16)
    w_s_o[...] = w_spatial_ref[...].astype(bf16)
    for hd in range(SGU_HEADS):
        hcols = slice(hd * SGU_HEAD_DIM, (hd + 1) * SGU_HEAD_DIM)
        w_rowsum = jnp.sum(w_spatial_ref[hd], axis=-1, keepdims=True)
        b_s_o[hd] = w_rowsum * ln_b_ref[:, hcols] + as_column(b_spatial_ref[hd:hd + 1, :])


def _tile_compute(seq_len, tiles_per_seq, j, tile,
                  x_ref, xprev_ref, xnext_ref, w_in_ref, ln_g_ref, w_s_ref, b_s_ref,
                  w_out_ref, final_g_ref, o_ref):
    f32, bf16 = jnp.float32, jnp.bfloat16
    n_chunks = tile // CHUNK
    pos0 = j * tile

    def proj(hh, col0, width):
        return jnp.dot(hh, w_in_ref[:, col0:col0 + width], preferred_element_type=f32)

    h = _normalize(x_ref[...]).astype(bf16)

    x_halo = jnp.concatenate([xprev_ref[...], xnext_ref[...]], axis=0)
    a_halo = proj(_normalize(x_halo).astype(bf16), COL_A, A_WIDTH)
    a_ext = jnp.concatenate([jnp.where(j > 0, a_halo[0:HALO], 0.0),
                             proj(h, COL_A, A_WIDTH),
                             jnp.where(j < tiles_per_seq - 1, a_halo[HALO:], 0.0)], axis=0)

    edge_rows = lax.broadcasted_iota(jnp.int32, (HALO, 1), 0)
    n_ext = tile + 2 * HALO
    sublane = lax.broadcasted_iota(jnp.int32, (1, SUBLANES, POOL_CH), 1)

    def shifted(x3, k):
        whole, part = divmod(k, SUBLANES)
        if whole:
            x3 = jnp.concatenate([x3[whole:], x3[:whole]], axis=0)
        if part:
            rot = pltpu.roll(x3, SUBLANES - part, 1)
            nxt = jnp.concatenate([rot[1:], rot[:1]], axis=0)
            x3 = jnp.where(sublane < SUBLANES - part, rot, nxt)
        return x3

    diffs = []
    for g, w in enumerate(POOL_WINDOWS):
        r = w // 2
        ext = a_ext[:, g * POOL_CH:(g + 1) * POOL_CH]
        run, span = ext.reshape(n_ext // SUBLANES, SUBLANES, POOL_CH), 1
        while span < r:
            run = run + shifted(run, span)
            span *= 2
        win = (run + shifted(run, n_ext - r)).reshape(n_ext, POOL_CH)

        def edge_diff(row0):
            t_abs = pos0 + row0 + edge_rows
            cnt = (jnp.minimum(t_abs + r, seq_len) - jnp.maximum(t_abs - r, 0)).astype(f32)
            rows = slice(HALO + row0, 2 * HALO + row0)
            return win[rows] / cnt - ext[rows]

        inner = slice(2 * HALO, tile)
        diffs.append(jnp.concatenate([edge_diff(0), win[inner] * (1.0 / w) - ext[inner],
                                      edge_diff(tile - HALO)], axis=0))

    half_gate_a = proj(h, COL_GATE_A, A_WIDTH)
    v = proj(h, COL_V, B_WIDTH)
    silu_a = _silu_of_half(half_gate_a)
    mix_parts = [(diffs[g] * silu_a[:, g * POOL_CH:(g + 1) * POOL_CH]).astype(bf16)
                 for g in range(len(POOL_WINDOWS))]

    mu = jnp.mean(v, axis=-1, keepdims=True)
    vc = v - mu
    var = jnp.mean(vc * vc, axis=-1, keepdims=True)
    vn = (vc * lax.rsqrt(var + EPS)).astype(bf16)
    half_gate_b = proj(h, COL_GATE_B, B_WIDTH)
    mixed_heads = []
    for hd in range(SGU_HEADS):
        hcols = slice(hd * SGU_HEAD_DIM, (hd + 1) * SGU_HEAD_DIM)
        vn_wide = jnp.concatenate(
            [vn[c * CHUNK:(c + 1) * CHUNK, hcols] for c in range(n_chunks)], axis=-1)
        mw = jnp.dot(w_s_ref[hd], vn_wide, preferred_element_type=f32)
        gain, shift = ln_g_ref[:, hcols], b_s_ref[hd]
        mixed_heads.append(jnp.concatenate(
            [mw[:, c * SGU_HEAD_DIM:(c + 1) * SGU_HEAD_DIM] * gain + shift
             for c in range(n_chunks)], axis=0))
    mixed = jnp.concatenate(mixed_heads, axis=-1)

    u = proj(h, COL_U, B_WIDTH)
    gated = mixed * _silu_of_half(half_gate_b)
    mix_all = jnp.concatenate(mix_parts + [(u * gated).astype(bf16)], axis=-1)

    final_g = final_g_ref[...]
    for rb in range(tile // OUT_ROWS):
        rows = slice(rb * OUT_ROWS, (rb + 1) * OUT_ROWS)
        mix = mix_all[rows, :]
        delta = jnp.concatenate(
            [jnp.dot(mix, w_out_ref[:, c0:c0 + OUT_COLS], preferred_element_type=f32)
             for c0 in range(0, D_MODEL, OUT_COLS)], axis=-1)
        o_ref[rows, :] = _normalize(x_ref[rows, :] + delta) * final_g


def _layers_kernel(tile, streams,
                   xa_hbm, xb_hbm, w_in_hbm, w_out_hbm, norm_g_ref, pool_w_ref, pool_scale_ref,
                   w_spatial_ref, b_spatial_ref, ln_g_ref, ln_b_ref, final_g_ref,
                   oa_hbm, ob_hbm,
                   x_buf, halo_buf, o_buf, x_sem, halo_sem, o_sem,
                   w_in_ref, w_s_ref, b_s_ref, w_out_ref, stage_in, stage_out, stage_sem):
    (n_a, seq_a), (n_b, seq_b) = streams
    n_steps = n_a + n_b

    def in_copies(hbm, n_tiles, t, to_slot):
        row0 = pl.multiple_of(t * tile, tile)
        before = pl.multiple_of(jnp.maximum(row0 - HALO, 0), HALO)
        after = pl.multiple_of(jnp.minimum(row0 + tile, n_tiles * tile - HALO), HALO)
        return (
            pltpu.make_async_copy(hbm.at[pl.ds(row0, tile), :], x_buf.at[to_slot],
                                  x_sem.at[to_slot]),
            pltpu.make_async_copy(hbm.at[pl.ds(before, HALO), :],
                                  halo_buf.at[to_slot, 0], halo_sem.at[to_slot, 0]),
            pltpu.make_async_copy(hbm.at[pl.ds(after, HALO), :],
                                  halo_buf.at[to_slot, 1], halo_sem.at[to_slot, 1]),
        )

    def out_copy(hbm, t, from_slot):
        return pltpu.make_async_copy(o_buf.at[from_slot],
                                     hbm.at[pl.ds(pl.multiple_of(t * tile, tile), tile), :],
                                     o_sem.at[from_slot])

    def start_fetch(step, to_slot):
        @pl.when(step < n_a)
        def _():
            for c in in_copies(xa_hbm, n_a, step, to_slot):
                c.start()

        @pl.when(step >= n_a)
        def _():
            for c in in_copies(xb_hbm, n_b, step - n_a, to_slot):
                c.start()

    start_fetch(jnp.int32(0), 0)
    _prepare_weights(norm_g_ref, pool_w_ref, pool_scale_ref, w_spatial_ref, b_spatial_ref,
                     ln_b_ref, w_in_hbm, w_out_hbm, stage_in, stage_out, stage_sem,
                     w_in_ref, w_s_ref, b_s_ref, w_out_ref)

    def step(i, carry):
        slot = lax.rem(i, 2)

        @pl.when(i >= 2)
        def _():
            out_copy(oa_hbm, 0, slot).wait()

        @pl.when(i + 1 < n_steps)
        def _():
            start_fetch(i + 1, 1 - slot)

        for c in in_copies(xa_hbm, n_a, 0, slot):
            c.wait()

        in_a = i < n_a
        t = jnp.where(in_a, i, i - n_a)
        seq_len = jnp.where(in_a, seq_a, seq_b)
        tiles_per_seq = jnp.where(in_a, seq_a // tile, seq_b // tile)
        _tile_compute(seq_len, tiles_per_seq, lax.rem(t, tiles_per_seq), tile,
                      x_buf.at[slot], halo_buf.at[slot, 0], halo_buf.at[slot, 1],
                      w_in_ref, ln_g_ref, w_s_ref, b_s_ref, w_out_ref, final_g_ref,
                      o_buf.at[slot])

        @pl.when(in_a)
        def _():
            out_copy(oa_hbm, t, slot).start()

        @pl.when(jnp.logical_not(in_a))
        def _():
            out_copy(ob_hbm, t, slot).start()

        return carry

    lax.fori_loop(0, n_steps, step, 0)
    out_copy(oa_hbm, 0, 0).wait()
    out_copy(oa_hbm, 0, 1).wait()


def _layers(xa, xb, norm_g, w_in, pool_w, pool_scale, sgu_ln_g, sgu_ln_b, w_spatial, b_spatial,
            w_out, final_g, tile):
    d = D_MODEL
    assert D_MODEL % PREP_ROWS == 0

    def stream(x):
        bsz, seq_len, dd = x.shape
        assert dd == d and seq_len % tile == 0 and tile % CHUNK == 0 and tile % OUT_ROWS == 0
        n_tiles = bsz * seq_len // tile
        assert n_tiles * tile >= 2 * HALO
        return x.reshape(bsz * seq_len, d), (n_tiles, seq_len)

    xa2, info_a = stream(xa)
    xb2, info_b = stream(xb)
    assert info_a[0] + info_b[0] >= 2

    anywhere = pl.BlockSpec(memory_space=pl.ANY)
    whole = pl.BlockSpec(memory_space=pltpu.VMEM)
    in_specs = [
        anywhere,
        anywhere,
        anywhere,
        anywhere,
        whole,
        whole,
        whole,
        whole,
        whole,
        whole,
        whole,
        whole,
    ]
    oa, ob = pl.pallas_call(
        functools.partial(_layers_kernel, tile, (info_a, info_b)),
        in_specs=in_specs,
        out_specs=[anywhere, anywhere],
        out_shape=[jax.ShapeDtypeStruct(xa2.shape, xa.dtype),
                   jax.ShapeDtypeStruct(xb2.shape, xb.dtype)],
        scratch_shapes=[pltpu.VMEM((2, tile, d), jnp.float32),
                        pltpu.VMEM((2, 2, HALO, d), jnp.float32),
                        pltpu.VMEM((2, tile, d), jnp.float32),
                        pltpu.SemaphoreType.DMA((2,)),
                        pltpu.SemaphoreType.DMA((2, 2)),
                        pltpu.SemaphoreType.DMA((2,)),
                        pltpu.VMEM((D_MODEL, IN_WIDTH), jnp.bfloat16),
                        pltpu.VMEM(w_spatial.shape, jnp.bfloat16),
                        pltpu.VMEM((SGU_HEADS, CHUNK, SGU_HEAD_DIM), jnp.float32),
                        pltpu.VMEM((D_MODEL, D_MODEL + W_OUT_PAD), jnp.bfloat16),
                        pltpu.VMEM((2, PREP_ROWS, IN_WIDTH), jnp.float32),
                        pltpu.VMEM((2, PREP_ROWS, D_MODEL), jnp.float32),
                        pltpu.SemaphoreType.DMA((2, 2))],
        compiler_params=pltpu.CompilerParams(vmem_limit_bytes=VMEM_LIMIT_BYTES),
    )(xa2, xb2, w_in, w_out, norm_g.reshape(1, D_MODEL), pool_w, pool_scale.reshape(1, A_WIDTH),
      w_spatial, b_spatial, sgu_ln_g.reshape(1, B_WIDTH), sgu_ln_b.reshape(1, B_WIDTH),
      final_g.reshape(1, D_MODEL))
    return oa.reshape(xa.shape), ob.reshape(xb.shape)


def kernel(x_prompt, x_sample, norm_g, w_in, pool_w, pool_scale, sgu_ln_g, sgu_ln_b,
           w_spatial, b_spatial, w_out, final_g):
    return _layers(x_prompt, x_sample, norm_g, w_in, pool_w, pool_scale, sgu_ln_g, sgu_ln_b,
                   w_spatial, b_spatial, w_out, final_g, TILE)
```

```python
import functools

import jax
import jax.numpy as jnp
from jax import lax
from jax.experimental import pallas as pl
from jax.experimental.pallas import tpu as pltpu

D_MODEL = 1024
A_WIDTH = 512
B_WIDTH = 512
POOL_WINDOWS = (2, 4, 8, 16)
POOL_CH = 128
CHUNK = 128
SGU_HEADS = 4
SGU_HEAD_DIM = 128
IN_WIDTH = 2 * A_WIDTH + 3 * B_WIDTH
EPS = 1e-6

COL_A, COL_GATE_A, COL_U, COL_V, COL_GATE_B = 0, 512, 1024, 1536, 2048

HALO = max(POOL_WINDOWS) // 2
SUBLANES = 8
TILE = 1024
W_OUT_PAD = 128
OUT_COLS = 512
OUT_ROWS = 256
PREP_ROWS = 256
VMEM_LIMIT_BYTES = 56 * 1024 * 1024


def _normalize(x):
    ms = jnp.mean(x * x, axis=-1, keepdims=True)
    return x * lax.rsqrt(ms + EPS)


def _silu_of_half(hx):
    return hx + hx * jnp.tanh(hx)


def _prepare_weights(norm_g_ref, pool_w_ref, pool_scale_ref, w_spatial_ref, b_spatial_ref, ln_b_ref,
                     w_in_hbm, w_out_hbm, stage_in, stage_out, stage_sem,
                     w_in_o, w_s_o, b_s_o, w_out_o):
    bf16 = jnp.bfloat16
    n_chunks = D_MODEL // PREP_ROWS

    def as_column(row):
        n = row.shape[-1]
        on_diag = (lax.broadcasted_iota(jnp.int32, (n, n), 0)
                   == lax.broadcasted_iota(jnp.int32, (n, n), 1))
        return jnp.sum(jnp.where(on_diag, row, 0.0), axis=-1, keepdims=True)

    def chunk_copies(k, slot):
        rows = pl.ds(k * PREP_ROWS, PREP_ROWS)
        return (pltpu.make_async_copy(w_in_hbm.at[rows, :], stage_in.at[slot], stage_sem.at[slot, 0]),
                pltpu.make_async_copy(w_out_hbm.at[rows, :], stage_out.at[slot], stage_sem.at[slot, 1]))

    for c in chunk_copies(0, 0):
        c.start()
    for k in range(n_chunks):
        slot = k % 2
        if k + 1 < n_chunks:
            for c in chunk_copies(k + 1, 1 - slot):
                c.start()
        for c in chunk_copies(k, slot):
            c.wait()
        rows = slice(k * PREP_ROWS, (k + 1) * PREP_ROWS)
        w_in_blk, w_out_blk = stage_in.at[slot], stage_out.at[slot]
        g = as_column(norm_g_ref[:, rows])
        for grp in range(len(POOL_WINDOWS)):
            cols = slice(grp * POOL_CH, (grp + 1) * POOL_CH)
            folded = jnp.dot(w_in_blk[:, cols] * g, pool_w_ref[grp],
                             preferred_element_type=jnp.float32,
                             precision=lax.Precision.HIGHEST)
            w_in_o[rows, cols] = (folded * pool_scale_ref[:, cols]).astype(bf16)
        for col0, width, col_scale in ((COL_GATE_A, A_WIDTH, 0.5), (COL_U, B_WIDTH, 1.0),
                                       (COL_V, B_WIDTH, 1.0), (COL_GATE_B, B_WIDTH, 0.5)):
            cols = slice(col0, col0 + width)
            w_in_o[rows, cols] = (w_in_blk[:, cols] * (col_scale * g)).astype(bf16)
        w_out_o[rows, :D_MODEL] = w_out_blk[...].astype(bf16)
        w_out_o[rows, D_MODEL:] = jnp.zeros((PREP_ROWS, W_OUT_PAD), bf16)
    w_s_o[...] = w_spatial_ref[...].astype(bf16)
    for hd in range(SGU_HEADS):
        hcols = slice(hd * SGU_HEAD_DIM, (hd + 1) * SGU_HEAD_DIM)
        w_rowsum = jnp.sum(w_spatial_ref[hd], axis=-1, keepdims=True)
        b_s_o[hd] = w_rowsum * ln_b_ref[:, hcols] + as_column(b_spatial_ref[hd:hd + 1, :])


def _tile_compute(seq_len, tiles_per_seq, j, tile,
                  x_ref, xprev_ref, xnext_ref, w_in_ref, ln_g_ref, w_s_ref, b_s_ref,
                  w_out_ref, final_g_ref, o_ref):
    f32, bf16 = jnp.float32, jnp.bfloat16
    n_chunks = tile // CHUNK
    pos0 = j * tile

    def proj(hh, col0, width):
        return jnp.dot(hh, w_in_ref[:, col0:col0 + width], preferred_element_type=f32)

    h = _normalize(x_ref[...]).astype(bf16)

    x_halo = jnp.concatenate([xprev_ref[...], xnext_ref[...]], axis=0)
    a_halo = proj(_normalize(x_halo).astype(bf16), COL_A, A_WIDTH)
    a_ext = jnp.concatenate([jnp.where(j > 0, a_halo[0:HALO], 0.0),
                             proj(h, COL_A, A_WIDTH),
                             jnp.where(j < tiles_per_seq - 1, a_halo[HALO:], 0.0)], axis=0)

    edge_rows = lax.broadcasted_iota(jnp.int32, (HALO, 1), 0)
    n_ext = tile + 2 * HALO
    sublane = lax.broadcasted_iota(jnp.int32, (1, SUBLANES, POOL_CH), 1)

    def shifted(x3, k):
        whole, part = divmod(k, SUBLANES)
        if whole:
            x3 = jnp.concatenate([x3[whole:], x3[:whole]], axis=0)
        if part:
            rot = pltpu.roll(x3, SUBLANES - part, 1)
            nxt = jnp.concatenate([rot[1:], rot[:1]], axis=0)
            x3 = jnp.where(sublane < SUBLANES - part, rot, nxt)
        return x3

    diffs = []
    for g, w in enumerate(POOL_WINDOWS):
        r = w // 2
        ext = a_ext[:, g * POOL_CH:(g + 1) * POOL_CH]
        run, span = ext.reshape(n_ext // SUBLANES, SUBLANES, POOL_CH), 1
        while span < r:
            run = run + shifted(run, span)
            span *= 2
        win = (run + shifted(run, n_ext - r)).reshape(n_ext, POOL_CH)

        def edge_diff(row0):
            t_abs = pos0 + row0 + edge_rows
            cnt = (jnp.minimum(t_abs + r, seq_len) - jnp.maximum(t_abs - r, 0)).astype(f32)
            rows = slice(HALO + row0, 2 * HALO + row0)
            return win[rows] / cnt - ext[rows]

        inner = slice(2 * HALO, tile)
        diffs.append(jnp.concatenate([edge_diff(0), win[inner] * (1.0 / w) - ext[inner],
                                      edge_diff(tile - HALO)], axis=0))

    half_gate_a = proj(h, COL_GATE_A, A_WIDTH)
    v = proj(h, COL_V, B_WIDTH)
    silu_a = _silu_of_half(half_gate_a)
    mix_parts = [(diffs[g] * silu_a[:, g * POOL_CH:(g + 1) * POOL_CH]).astype(bf16)
                 for g in range(len(POOL_WINDOWS))]

    mu = jnp.mean(v, axis=-1, keepdims=True)
    vc = v - mu
    var = jnp.mean(vc * vc, axis=-1, keepdims=True)
    vn = (vc * lax.rsqrt(var + EPS)).astype(bf16)
    half_gate_b = proj(h, COL_GATE_B, B_WIDTH)
    mixed_heads = []
    for hd in range(SGU_HEADS):
        hcols = slice(hd * SGU_HEAD_DIM, (hd + 1) * SGU_HEAD_DIM)
        vn_wide = jnp.concatenate(
            [vn[c * CHUNK:(c + 1) * CHUNK, hcols] for c in range(n_chunks)], axis=-1)
        mw = jnp.dot(w_s_ref[hd], vn_wide, preferred_element_type=f32)
        gain, shift = ln_g_ref[:, hcols], b_s_ref[hd]
        mixed_heads.append(jnp.concatenate(
            [mw[:, c * SGU_HEAD_DIM:(c + 1) * SGU_HEAD_DIM] * gain + shift
             for c in range(n_chunks)], axis=0))
    mixed = jnp.concatenate(mixed_heads, axis=-1)

    u = proj(h, COL_U, B_WIDTH)
    gated = mixed * _silu_of_half(half_gate_b)
    mix_all = jnp.concatenate(mix_parts + [(u * gated).astype(bf16)], axis=-1)

    final_g = final_g_ref[...]
    for rb in range(tile // OUT_ROWS):
        rows = slice(rb * OUT_ROWS, (rb + 1) * OUT_ROWS)
        mix = mix_all[rows, :]
        delta = jnp.concatenate(
            [jnp.dot(mix, w_out_ref[:, c0:c0 + OUT_COLS], preferred_element_type=f32)
             for c0 in range(0, D_MODEL, OUT_COLS)], axis=-1)
        o_ref[rows, :] = _normalize(x_ref[rows, :] + delta) * final_g


def _layers_kernel(tile, streams,
                   xa_hbm, xb_hbm, w_in_hbm, w_out_hbm, norm_g_ref, pool_w_ref, pool_scale_ref,
                   w_spatial_ref, b_spatial_ref, ln_g_ref, ln_b_ref, final_g_ref,
                   oa_hbm, ob_hbm,
                   x_buf, halo_buf, o_buf, x_sem, halo_sem, o_sem,
                   w_in_ref, w_s_ref, b_s_ref, w_out_ref, stage_in, stage_out, stage_sem):
    (n_a, seq_a), (n_b, seq_b) = streams

    def in_copies(hbm, n_tiles, t, to_slot):
        row0 = pl.multiple_of(t * tile, tile)
        before = pl.multiple_of(jnp.maximum(row0 - HALO, 0), HALO)
        after = pl.multiple_of(jnp.minimum(row0 + tile, n_tiles * tile - HALO), HALO)
        return (
            pltpu.make_async_copy(hbm.at[pl.ds(row0, tile), :], x_buf.at[to_slot],
                                  x_sem.at[to_slot]),
            pltpu.make_async_copy(hbm.at[pl.ds(before, HALO), :],
                                  halo_buf.at[to_slot, 0], halo_sem.at[to_slot, 0]),
            pltpu.make_async_copy(hbm.at[pl.ds(after, HALO), :],
                                  halo_buf.at[to_slot, 1], halo_sem.at[to_slot, 1]),
        )

    def out_copy(hbm, t, from_slot):
        return pltpu.make_async_copy(o_buf.at[from_slot],
                                     hbm.at[pl.ds(pl.multiple_of(t * tile, tile), tile), :],
                                     o_sem.at[from_slot])

    def run_stream(x_hbm, o_hbm, n_tiles, seq_len, first_step, first_slot):
        tiles_per_seq = seq_len // tile

        def step(t, carry):
            g = first_step + t
            oslot = lax.rem(g, 2)
            slot = jnp.where(t == 0, first_slot, oslot)
            if first_step >= 2:
                out_copy(o_hbm, 0, oslot).wait()
            else:
                @pl.when(g >= 2)
                def _():
                    out_copy(o_hbm, 0, oslot).wait()
            for c in in_copies(x_hbm, n_tiles, jnp.minimum(t + 1, n_tiles - 1), 1 - oslot):
                c.start()
            for c in in_copies(x_hbm, n_tiles, 0, slot):
                c.wait()
            _tile_compute(seq_len, tiles_per_seq, lax.rem(t, tiles_per_seq), tile,
                          x_buf.at[slot], halo_buf.at[slot, 0], halo_buf.at[slot, 1],
                          w_in_ref, ln_g_ref, w_s_ref, b_s_ref, w_out_ref, final_g_ref,
                          o_buf.at[oslot])
            out_copy(o_hbm, t, oslot).start()
            return carry

        lax.fori_loop(0, n_tiles, step, 0)
        for c in in_copies(x_hbm, n_tiles, 0, (first_step + n_tiles) % 2):
            c.wait()

    for c in in_copies(xa_hbm, n_a, 0, 0):
        c.start()
    _prepare_weights(norm_g_ref, pool_w_ref, pool_scale_ref, w_spatial_ref, b_spatial_ref,
                     ln_b_ref, w_in_hbm, w_out_hbm, stage_in, stage_out, stage_sem,
                     w_in_ref, w_s_ref, b_s_ref, w_out_ref)
    for c in in_copies(xb_hbm, n_b, 0, 2):
        c.start()
    run_stream(xa_hbm, oa_hbm, n_a, seq_a, 0, 0)
    run_stream(xb_hbm, ob_hbm, n_b, seq_b, n_a, 2)
    out_copy(oa_hbm, 0, 0).wait()
    out_copy(oa_hbm, 0, 1).wait()


def _layers(xa, xb, norm_g, w_in, pool_w, pool_scale, sgu_ln_g, sgu_ln_b, w_spatial, b_spatial,
            w_out, final_g, tile):
    d = D_MODEL
    assert D_MODEL % PREP_ROWS == 0

    def stream(x):
        bsz, seq_len, dd = x.shape
        assert dd == d and seq_len % tile == 0 and tile % CHUNK == 0 and tile % OUT_ROWS == 0
        n_tiles = bsz * seq_len // tile
        assert n_tiles * tile >= 2 * HALO
        return x.reshape(bsz * seq_len, d), (n_tiles, seq_len)

    xa2, info_a = stream(xa)
    xb2, info_b = stream(xb)
    assert info_a[0] + info_b[0] >= 2

    anywhere = pl.BlockSpec(memory_space=pl.ANY)
    whole = pl.BlockSpec(memory_space=pltpu.VMEM)
    in_specs = [
        anywhere,
        anywhere,
        anywhere,
        anywhere,
        whole,
        whole,
        whole,
        whole,
        whole,
        whole,
        whole,
        whole,
    ]
    oa, ob = pl.pallas_call(
        functools.partial(_layers_kernel, tile, (info_a, info_b)),
        in_specs=in_specs,
        out_specs=[anywhere, anywhere],
        out_shape=[jax.ShapeDtypeStruct(xa2.shape, xa.dtype),
                   jax.ShapeDtypeStruct(xb2.shape, xb.dtype)],
        scratch_shapes=[pltpu.VMEM((3, tile, d), jnp.float32),
                        pltpu.VMEM((3, 2, HALO, d), jnp.float32),
                        pltpu.VMEM((2, tile, d), jnp.float32),
                        pltpu.SemaphoreType.DMA((3,)),
                        pltpu.SemaphoreType.DMA((3, 2)),
                        pltpu.SemaphoreType.DMA((2,)),
                        pltpu.VMEM((D_MODEL, IN_WIDTH), jnp.bfloat16),
                        pltpu.VMEM(w_spatial.shape, jnp.bfloat16),
                        pltpu.VMEM((SGU_HEADS, CHUNK, SGU_HEAD_DIM), jnp.float32),
                        pltpu.VMEM((D_MODEL, D_MODEL + W_OUT_PAD), jnp.bfloat16),
                        pltpu.VMEM((2, PREP_ROWS, IN_WIDTH), jnp.float32),
                        pltpu.VMEM((2, PREP_ROWS, D_MODEL), jnp.float32),
                        pltpu.SemaphoreType.DMA((2, 2))],
        compiler_params=pltpu.CompilerParams(vmem_limit_bytes=VMEM_LIMIT_BYTES),
    )(xa2, xb2, w_in, w_out, norm_g.reshape(1, D_MODEL), pool_w, pool_scale.reshape(1, A_WIDTH),
      w_spatial, b_spatial, sgu_ln_g.reshape(1, B_WIDTH), sgu_ln_b.reshape(1, B_WIDTH),
      final_g.reshape(1, D_MODEL))
    return oa.reshape(xa.shape), ob.reshape(xb.shape)


def kernel(x_prompt, x_sample, norm_g, w_in, pool_w, pool_scale, sgu_ln_g, sgu_ln_b,
           w_spatial, b_spatial, w_out, final_g):
    return _layers(x_prompt, x_sample, norm_g, w_in, pool_w, pool_scale, sgu_ln_g, sgu_ln_b,
                   w_spatial, b_spatial, w_out, final_g, TILE)
```

```python
import functools

import jax
import jax.numpy as jnp
from jax import lax
from jax.experimental import pallas as pl
from jax.experimental.pallas import tpu as pltpu

D_MODEL = 1024
A_WIDTH = 512
B_WIDTH = 512
POOL_WINDOWS = (2, 4, 8, 16)
POOL_CH = 128
CHUNK = 128
SGU_HEADS = 4
SGU_HEAD_DIM = 128
IN_WIDTH = 2 * A_WIDTH + 3 * B_WIDTH
EPS = 1e-6

COL_A, COL_GATE_A, COL_U, COL_V, COL_GATE_B = 0, 512, 1024, 1536, 2048

HALO = max(POOL_WINDOWS) // 2
SUBLANES = 8
TILE = 1024
W_OUT_PAD = 128
OUT_COLS = 512
OUT_ROWS = 256
PREP_ROWS = 256
VMEM_LIMIT_BYTES = 56 * 1024 * 1024


def _normalize(x):
    ms = jnp.mean(x * x, axis=-1, keepdims=True)
    return x * lax.rsqrt(ms + EPS)


def _silu_of_half(hx):
    return hx + hx * jnp.tanh(hx)


def _prepare_weights(norm_g_ref, pool_w_ref, pool_scale_ref, w_spatial_ref, b_spatial_ref, ln_b_ref,
                     w_in_hbm, w_out_hbm, stage_in, stage_out, stage_sem,
                     w_in_o, w_s_o, b_s_o, w_out_o, start_first_tile):
    bf16 = jnp.bfloat16
    n_chunks = D_MODEL // PREP_ROWS

    def as_column(row):
        n = row.shape[-1]
        on_diag = (lax.broadcasted_iota(jnp.int32, (n, n), 0)
                   == lax.broadcasted_iota(jnp.int32, (n, n), 1))
        return jnp.sum(jnp.where(on_diag, row, 0.0), axis=-1, keepdims=True)

    def chunk_copies(k, slot):
        rows = pl.ds(k * PREP_ROWS, PREP_ROWS)
        return (pltpu.make_async_copy(w_in_hbm.at[rows, :], stage_in.at[slot], stage_sem.at[slot, 0]),
                pltpu.make_async_copy(w_out_hbm.at[rows, :], stage_out.at[slot], stage_sem.at[slot, 1]))

    for c in chunk_copies(0, 0):
        c.start()
    for k in range(n_chunks):
        slot = k % 2
        if k + 1 < n_chunks:
            for c in chunk_copies(k + 1, 1 - slot):
                c.start()
        if k == 0:
            start_first_tile()
        for c in chunk_copies(k, slot):
            c.wait()
        rows = slice(k * PREP_ROWS, (k + 1) * PREP_ROWS)
        w_in_blk, w_out_blk = stage_in.at[slot], stage_out.at[slot]
        g = as_column(norm_g_ref[:, rows])
        for grp in range(len(POOL_WINDOWS)):
            cols = slice(grp * POOL_CH, (grp + 1) * POOL_CH)
            folded = jnp.dot(w_in_blk[:, cols] * g, pool_w_ref[grp],
                             preferred_element_type=jnp.float32,
                             precision=lax.Precision.HIGHEST)
            w_in_o[rows, cols] = (folded * pool_scale_ref[:, cols]).astype(bf16)
        for col0, width, col_scale in ((COL_GATE_A, A_WIDTH, 0.5), (COL_U, B_WIDTH, 1.0),
                                       (COL_V, B_WIDTH, 1.0), (COL_GATE_B, B_WIDTH, 0.5)):
            cols = slice(col0, col0 + width)
            w_in_o[rows, cols] = (w_in_blk[:, cols] * (col_scale * g)).astype(bf16)
        w_out_o[rows, :D_MODEL] = w_out_blk[...].astype(bf16)
        w_out_o[rows, D_MODEL:] = jnp.zeros((PREP_ROWS, W_OUT_PAD), bf16)
    w_s_o[...] = w_spatial_ref[...].astype(bf16)
    for hd in range(SGU_HEADS):
        hcols = slice(hd * SGU_HEAD_DIM, (hd + 1) * SGU_HEAD_DIM)
        w_rowsum = jnp.sum(w_spatial_ref[hd], axis=-1, keepdims=True)
        b_s_o[hd] = w_rowsum * ln_b_ref[:, hcols] + as_column(b_spatial_ref[hd:hd + 1, :])


def _tile_compute(seq_len, tiles_per_seq, j, tile,
                  x_ref, xprev_ref, xnext_ref, w_in_ref, ln_g_ref, w_s_ref, b_s_ref,
                  w_out_ref, final_g_ref, o_ref):
    f32, bf16 = jnp.float32, jnp.bfloat16
    n_chunks = tile // CHUNK
    pos0 = j * tile

    def proj(hh, col0, width):
        return jnp.dot(hh, w_in_ref[:, col0:col0 + width], preferred_element_type=f32)

    h = _normalize(x_ref[...]).astype(bf16)

    x_halo = jnp.concatenate([xprev_ref[...], xnext_ref[...]], axis=0)
    a_halo = proj(_normalize(x_halo).astype(bf16), COL_A, A_WIDTH)
    a_ext = jnp.concatenate([jnp.where(j > 0, a_halo[0:HALO], 0.0),
                             proj(h, COL_A, A_WIDTH),
                             jnp.where(j < tiles_per_seq - 1, a_halo[HALO:], 0.0)], axis=0)

    edge_rows = lax.broadcasted_iota(jnp.int32, (HALO, 1), 0)
    n_ext = tile + 2 * HALO
    sublane = lax.broadcasted_iota(jnp.int32, (1, SUBLANES, POOL_CH), 1)

    def shifted(x3, k):
        whole, part = divmod(k, SUBLANES)
        if whole:
            x3 = jnp.concatenate([x3[whole:], x3[:whole]], axis=0)
        if part:
            rot = pltpu.roll(x3, SUBLANES - part, 1)
            nxt = jnp.concatenate([rot[1:], rot[:1]], axis=0)
            x3 = jnp.where(sublane < SUBLANES - part, rot, nxt)
        return x3

    diffs = []
    for g, w in enumerate(POOL_WINDOWS):
        r = w // 2
        ext = a_ext[:, g * POOL_CH:(g + 1) * POOL_CH]
        run, span = ext.reshape(n_ext // SUBLANES, SUBLANES, POOL_CH), 1
        while span < r:
            run = run + shifted(run, span)
            span *= 2
        win = (run + shifted(run, n_ext - r)).reshape(n_ext, POOL_CH)

        def edge_diff(row0):
            t_abs = pos0 + row0 + edge_rows
            cnt = (jnp.minimum(t_abs + r, seq_len) - jnp.maximum(t_abs - r, 0)).astype(f32)
            rows = slice(HALO + row0, 2 * HALO + row0)
            return win[rows] / cnt - ext[rows]

        inner = slice(2 * HALO, tile)
        diffs.append(jnp.concatenate([edge_diff(0), win[inner] * (1.0 / w) - ext[inner],
                                      edge_diff(tile - HALO)], axis=0))

    half_gate_a = proj(h, COL_GATE_A, A_WIDTH)
    v = proj(h, COL_V, B_WIDTH)
    silu_a = _silu_of_half(half_gate_a)
    mix_parts = [(diffs[g] * silu_a[:, g * POOL_CH:(g + 1) * POOL_CH]).astype(bf16)
                 for g in range(len(POOL_WINDOWS))]

    mu = jnp.mean(v, axis=-1, keepdims=True)
    vc = v - mu
    var = jnp.mean(vc * vc, axis=-1, keepdims=True)
    vn = (vc * lax.rsqrt(var + EPS)).astype(bf16)
    half_gate_b = proj(h, COL_GATE_B, B_WIDTH)
    mixed_heads = []
    for hd in range(SGU_HEADS):
        hcols = slice(hd * SGU_HEAD_DIM, (hd + 1) * SGU_HEAD_DIM)
        vn_wide = jnp.concatenate(
            [vn[c * CHUNK:(c + 1) * CHUNK, hcols] for c in range(n_chunks)], axis=-1)
        mw = jnp.dot(w_s_ref[hd], vn_wide, preferred_element_type=f32)
        gain, shift = ln_g_ref[:, hcols], b_s_ref[hd]
        mixed_heads.append(jnp.concatenate(
            [mw[:, c * SGU_HEAD_DIM:(c + 1) * SGU_HEAD_DIM] * gain + shift
             for c in range(n_chunks)], axis=0))
    mixed = jnp.concatenate(mixed_heads, axis=-1)

    u = proj(h, COL_U, B_WIDTH)
    gated = mixed * _silu_of_half(half_gate_b)
    mix_all = jnp.concatenate(mix_parts + [(u * gated).astype(bf16)], axis=-1)

    final_g = final_g_ref[...]
    for rb in range(tile // OUT_ROWS):
        rows = slice(rb * OUT_ROWS, (rb + 1) * OUT_ROWS)
        mix = mix_all[rows, :]
        delta = jnp.concatenate(
            [jnp.dot(mix, w_out_ref[:, c0:c0 + OUT_COLS], preferred_element_type=f32)
             for c0 in range(0, D_MODEL, OUT_COLS)], axis=-1)
        o_ref[rows, :] = _normalize(x_ref[rows, :] + delta) * final_g


def _layers_kernel(tile, streams,
                   xa_hbm, xb_hbm, w_in_hbm, w_out_hbm, norm_g_ref, pool_w_ref, pool_scale_ref,
                   w_spatial_ref, b_spatial_ref, ln_g_ref, ln_b_ref, final_g_ref,
                   oa_hbm, ob_hbm,
                   x_buf, halo_buf, o_buf, x_sem, halo_sem, o_sem,
                   w_in_ref, w_s_ref, b_s_ref, w_out_ref, stage_in, stage_out, stage_sem):
    (n_a, seq_a), (n_b, seq_b) = streams
    n_steps = n_a + n_b

    def in_copies(hbm, n_tiles, t, to_slot):
        row0 = pl.multiple_of(t * tile, tile)
        before = pl.multiple_of(jnp.maximum(row0 - HALO, 0), HALO)
        after = pl.multiple_of(jnp.minimum(row0 + tile, n_tiles * tile - HALO), HALO)
        return (
            pltpu.make_async_copy(hbm.at[pl.ds(row0, tile), :], x_buf.at[to_slot],
                                  x_sem.at[to_slot]),
            pltpu.make_async_copy(hbm.at[pl.ds(before, HALO), :],
                                  halo_buf.at[to_slot, 0], halo_sem.at[to_slot, 0]),
            pltpu.make_async_copy(hbm.at[pl.ds(after, HALO), :],
                                  halo_buf.at[to_slot, 1], halo_sem.at[to_slot, 1]),
        )

    def out_copy(hbm, t, from_slot):
        return pltpu.make_async_copy(o_buf.at[from_slot],
                                     hbm.at[pl.ds(pl.multiple_of(t * tile, tile), tile), :],
                                     o_sem.at[from_slot])

    def start_fetch(step, to_slot):
        @pl.when(step < n_a)
        def _():
            for c in in_copies(xa_hbm, n_a, step, to_slot):
                c.start()

        @pl.when(step >= n_a)
        def _():
            for c in in_copies(xb_hbm, n_b, step - n_a, to_slot):
                c.start()

    _prepare_weights(norm_g_ref, pool_w_ref, pool_scale_ref, w_spatial_ref, b_spatial_ref,
                     ln_b_ref, w_in_hbm, w_out_hbm, stage_in, stage_out, stage_sem,
                     w_in_ref, w_s_ref, b_s_ref, w_out_ref,
                     lambda: start_fetch(jnp.int32(0), 0))

    def step(i, carry):
        slot = lax.rem(i, 2)

        @pl.when(i >= 2)
        def _():
            out_copy(oa_hbm, 0, slot).wait()

        @pl.when(i + 1 < n_steps)
        def _():
            start_fetch(i + 1, 1 - slot)

        for c in in_copies(xa_hbm, n_a, 0, slot):
            c.wait()

        in_a = i < n_a
        t = jnp.where(in_a, i, i - n_a)
        seq_len = jnp.where(in_a, seq_a, seq_b)
        tiles_per_seq = jnp.where(in_a, seq_a // tile, seq_b // tile)
        _tile_compute(seq_len, tiles_per_seq, lax.rem(t, tiles_per_seq), tile,
                      x_buf.at[slot], halo_buf.at[slot, 0], halo_buf.at[slot, 1],
                      w_in_ref, ln_g_ref, w_s_ref, b_s_ref, w_out_ref, final_g_ref,
                      o_buf.at[slot])

        @pl.when(in_a)
        def _():
            out_copy(oa_hbm, t, slot).start()

        @pl.when(jnp.logical_not(in_a))
        def _():
            out_copy(ob_hbm, t, slot).start()

        return carry

    lax.fori_loop(0, n_steps, step, 0)
    out_copy(oa_hbm, 0, 0).wait()
    out_copy(oa_hbm, 0, 1).wait()


def _layers(xa, xb, norm_g, w_in, pool_w, pool_scale, sgu_ln_g, sgu_ln_b, w_spatial, b_spatial,
            w_out, final_g, tile):
    d = D_MODEL
    assert D_MODEL % PREP_ROWS == 0

    def stream(x):
        bsz, seq_len, dd = x.shape
        assert dd == d and seq_len % tile == 0 and tile % CHUNK == 0 and tile % OUT_ROWS == 0
        n_tiles = bsz * seq_len // tile
        assert n_tiles * tile >= 2 * HALO
        return x.reshape(bsz * seq_len, d), (n_tiles, seq_len)

    xa2, info_a = stream(xa)
    xb2, info_b = stream(xb)
    assert info_a[0] + info_b[0] >= 2

    anywhere = pl.BlockSpec(memory_space=pl.ANY)
    whole = pl.BlockSpec(memory_space=pltpu.VMEM)
    in_specs = [
        anywhere,
        anywhere,
        anywhere,
        anywhere,
        whole,
        whole,
        whole,
        whole,
        whole,
        whole,
        whole,
        whole,
    ]
    oa, ob = pl.pallas_call(
        functools.partial(_layers_kernel, tile, (info_a, info_b)),
        in_specs=in_specs,
        out_specs=[anywhere, anywhere],
        out_shape=[jax.ShapeDtypeStruct(xa2.shape, xa.dtype),
                   jax.ShapeDtypeStruct(xb2.shape, xb.dtype)],
        scratch_shapes=[pltpu.VMEM((2, tile, d), jnp.float32),
                        pltpu.VMEM((2, 2, HALO, d), jnp.float32),
                        pltpu.VMEM((2, tile, d), jnp.float32),
                        pltpu.SemaphoreType.DMA((2,)),
                        pltpu.SemaphoreType.DMA((2, 2)),
                        pltpu.SemaphoreType.DMA((2,)),
                        pltpu.VMEM((D_MODEL, IN_WIDTH), jnp.bfloat16),
                        pltpu.VMEM(w_spatial.shape, jnp.bfloat16),
                        pltpu.VMEM((SGU_HEADS, CHUNK, SGU_HEAD_DIM), jnp.float32),
                        pltpu.VMEM((D_MODEL, D_MODEL + W_OUT_PAD), jnp.bfloat16),
                        pltpu.VMEM((2, PREP_ROWS, IN_WIDTH), jnp.float32),
                        pltpu.VMEM((2, PREP_ROWS, D_MODEL), jnp.float32),
                        pltpu.SemaphoreType.DMA((2, 2))],
        compiler_params=pltpu.CompilerParams(vmem_limit_bytes=VMEM_LIMIT_BYTES),
    )(xa2, xb2, w_in, w_out, norm_g.reshape(1, D_MODEL), pool_w, pool_scale.reshape(1, A_WIDTH),
      w_spatial, b_spatial, sgu_ln_g.reshape(1, B_WIDTH), sgu_ln_b.reshape(1, B_WIDTH),
      final_g.reshape(1, D_MODEL))
    return oa.reshape(xa.shape), ob.reshape(xb.shape)


def kernel(x_prompt, x_sample, norm_g, w_in, pool_w, pool_scale, sgu_ln_g, sgu_ln_b,
           w_spatial, b_spatial, w_out, final_g):
    return _layers(x_prompt, x_sample, norm_g, w_in, pool_w, pool_scale, sgu_ln_g, sgu_ln_b,
                   w_spatial, b_spatial, w_out, final_g, TILE)
```

```python
import functools

import jax
import jax.numpy as jnp
from jax import lax
from jax.experimental import pallas as pl
from jax.experimental.pallas import tpu as pltpu

D_MODEL = 1024
A_WIDTH = 512
B_WIDTH = 512
POOL_WINDOWS = (2, 4, 8, 16)
POOL_CH = 128
CHUNK = 128
SGU_HEADS = 4
SGU_HEAD_DIM = 128
IN_WIDTH = 2 * A_WIDTH + 3 * B_WIDTH
EPS = 1e-6

COL_A, COL_GATE_A, COL_U, COL_V, COL_GATE_B = 0, 512, 1024, 1536, 2048

HALO = max(POOL_WINDOWS) // 2
SUBLANES = 8
TILE = 1024
W_OUT_PAD = 128
OUT_COLS = 512
OUT_ROWS = 256
PREP_ROWS = 256
VMEM_LIMIT_BYTES = 56 * 1024 * 1024


def _normalize(x):
    ms = jnp.mean(x * x, axis=-1, keepdims=True)
    return x * lax.rsqrt(ms + EPS)


def _silu_of_half(hx):
    return hx + hx * jnp.tanh(hx)


def _prepare_weights(norm_g_ref, pool_w_ref, pool_scale_ref, w_spatial_ref, b_spatial_ref, ln_b_ref,
                     w_in_hbm, w_out_hbm, stage_in, stage_out, stage_sem,
                     w_in_o, w_s_o, b_s_o, w_out_o, start_first_tile):
    bf16 = jnp.bfloat16
    n_chunks = D_MODEL // PREP_ROWS

    def as_column(row):
        n = row.shape[-1]
        on_diag = (lax.broadcasted_iota(jnp.int32, (n, n), 0)
                   == lax.broadcasted_iota(jnp.int32, (n, n), 1))
        return jnp.sum(jnp.where(on_diag, row, 0.0), axis=-1, keepdims=True)

    def chunk_copies(k, slot):
        rows = pl.ds(k * PREP_ROWS, PREP_ROWS)
        return (pltpu.make_async_copy(w_in_hbm.at[rows, :], stage_in.at[slot], stage_sem.at[slot, 0]),
                pltpu.make_async_copy(w_out_hbm.at[rows, :], stage_out.at[slot], stage_sem.at[slot, 1]))

    for c in chunk_copies(0, 0):
        c.start()
    for k in range(n_chunks):
        slot = k % 2
        if k + 1 < n_chunks:
            for c in chunk_copies(k + 1, 1 - slot):
                c.start()
        if k == 0:
            start_first_tile()
        for c in chunk_copies(k, slot):
            c.wait()
        rows = slice(k * PREP_ROWS, (k + 1) * PREP_ROWS)
        w_in_blk, w_out_blk = stage_in.at[slot], stage_out.at[slot]
        g = as_column(norm_g_ref[:, rows])
        for grp in range(len(POOL_WINDOWS)):
            cols = slice(grp * POOL_CH, (grp + 1) * POOL_CH)
            folded = jnp.dot(w_in_blk[:, cols] * g, pool_w_ref[grp],
                             preferred_element_type=jnp.float32,
                             precision=lax.Precision.HIGHEST)
            w_in_o[rows, cols] = (folded * pool_scale_ref[:, cols]).astype(bf16)
        for col0, width, col_scale in ((COL_GATE_A, A_WIDTH, 0.5), (COL_U, B_WIDTH, 1.0),
                                       (COL_V, B_WIDTH, 1.0), (COL_GATE_B, B_WIDTH, 0.5)):
            cols = slice(col0, col0 + width)
            w_in_o[rows, cols] = (w_in_blk[:, cols] * (col_scale * g)).astype(bf16)
        w_out_o[rows, :D_MODEL] = w_out_blk[...].astype(bf16)
        w_out_o[rows, D_MODEL:] = jnp.zeros((PREP_ROWS, W_OUT_PAD), bf16)
    w_s_o[...] = w_spatial_ref[...].astype(bf16)
    for hd in range(SGU_HEADS):
        hcols = slice(hd * SGU_HEAD_DIM, (hd + 1) * SGU_HEAD_DIM)
        w_rowsum = jnp.sum(w_spatial_ref[hd], axis=-1, keepdims=True)
        b_s_o[hd] = w_rowsum * ln_b_ref[:, hcols] + as_column(b_spatial_ref[hd:hd + 1, :])


def _tile_compute(seq_len, tiles_per_seq, j, tile,
                  x_ref, xprev_ref, xnext_ref, w_in_ref, ln_g_ref, w_s_ref, b_s_ref,
                  w_out_ref, final_g_ref, o_ref):
    f32, bf16 = jnp.float32, jnp.bfloat16
    n_chunks = tile // CHUNK
    pos0 = j * tile

    def proj(hh, col0, width):
        return jnp.dot(hh, w_in_ref[:, col0:col0 + width], preferred_element_type=f32)

    h = _normalize(x_ref[...]).astype(bf16)

    x_halo = jnp.concatenate([xprev_ref[...], xnext_ref[...]], axis=0)
    a_halo = proj(_normalize(x_halo).astype(bf16), COL_A, A_WIDTH)
    a_ext = jnp.concatenate([jnp.where(j > 0, a_halo[0:HALO], 0.0),
                             proj(h, COL_A, A_WIDTH),
                             jnp.where(j < tiles_per_seq - 1, a_halo[HALO:], 0.0)], axis=0)

    edge_rows = lax.broadcasted_iota(jnp.int32, (HALO, 1), 0)
    n_ext = tile + 2 * HALO
    sublane = lax.broadcasted_iota(jnp.int32, (1, SUBLANES, POOL_CH), 1)

    def shifted(x3, k):
        whole, part = divmod(k, SUBLANES)
        if whole:
            x3 = jnp.concatenate([x3[whole:], x3[:whole]], axis=0)
        if part:
            rot = pltpu.roll(x3, SUBLANES - part, 1)
            nxt = jnp.concatenate([rot[1:], rot[:1]], axis=0)
            x3 = jnp.where(sublane < SUBLANES - part, rot, nxt)
        return x3

    diffs = []
    for g, w in enumerate(POOL_WINDOWS):
        r = w // 2
        ext = a_ext[:, g * POOL_CH:(g + 1) * POOL_CH]
        run, span = ext.reshape(n_ext // SUBLANES, SUBLANES, POOL_CH), 1
        while span < r:
            run = run + shifted(run, span)
            span *= 2
        win = (run + shifted(run, n_ext - r)).reshape(n_ext, POOL_CH)

        def edge_diff(row0):
            t_abs = pos0 + row0 + edge_rows
            cnt = (jnp.minimum(t_abs + r, seq_len) - jnp.maximum(t_abs - r, 0)).astype(f32)
            rows = slice(HALO + row0, 2 * HALO + row0)
            return win[rows] / cnt - ext[rows]

        inner = slice(2 * HALO, tile)
        diffs.append(jnp.concatenate([edge_diff(0), win[inner] * (1.0 / w) - ext[inner],
                                      edge_diff(tile - HALO)], axis=0))

    half_gate_a = proj(h, COL_GATE_A, A_WIDTH)
    v = proj(h, COL_V, B_WIDTH)
    silu_a = _silu_of_half(half_gate_a)
    mix_parts = [(diffs[g] * silu_a[:, g * POOL_CH:(g + 1) * POOL_CH]).astype(bf16)
                 for g in range(len(POOL_WINDOWS))]

    mu = jnp.mean(v, axis=-1, keepdims=True)
    vc = v - mu
    var = jnp.mean(vc * vc, axis=-1, keepdims=True)
    vn = (vc * lax.rsqrt(var + EPS)).astype(bf16)
    half_gate_b = proj(h, COL_GATE_B, B_WIDTH)
    mixed_heads = []
    for hd in range(SGU_HEADS):
        hcols = slice(hd * SGU_HEAD_DIM, (hd + 1) * SGU_HEAD_DIM)
        vn_wide = jnp.concatenate(
            [vn[c * CHUNK:(c + 1) * CHUNK, hcols] for c in range(n_chunks)], axis=-1)
        mw = jnp.dot(w_s_ref[hd], vn_wide, preferred_element_type=f32)
        gain, shift = ln_g_ref[:, hcols], b_s_ref[hd]
        mixed_heads.append(jnp.concatenate(
            [mw[:, c * SGU_HEAD_DIM:(c + 1) * SGU_HEAD_DIM] * gain + shift
             for c in range(n_chunks)], axis=0))
    mixed = jnp.concatenate(mixed_heads, axis=-1)

    u = proj(h, COL_U, B_WIDTH)
    gated = mixed * _silu_of_half(half_gate_b)
    mix_all = jnp.concatenate(mix_parts + [(u * gated).astype(bf16)], axis=-1)

    final_g = final_g_ref[...]
    for rb in range(tile // OUT_ROWS):
        rows = slice(rb * OUT_ROWS, (rb + 1) * OUT_ROWS)
        mix = mix_all[rows, :]
        delta = jnp.concatenate(
            [jnp.dot(mix, w_out_ref[:, c0:c0 + OUT_COLS], preferred_element_type=f32)
             for c0 in range(0, D_MODEL, OUT_COLS)], axis=-1)
        o_ref[rows, :] = _normalize(x_ref[rows, :] + delta) * final_g


def _layers_kernel(tile, streams,
                   xa_hbm, xb_hbm, w_in_hbm, w_out_hbm, norm_g_ref, pool_w_ref, pool_scale_ref,
                   w_spatial_ref, b_spatial_ref, ln_g_ref, ln_b_ref, final_g_ref,
                   oa_hbm, ob_hbm,
                   x_buf, halo_buf, o_buf, x_sem, halo_sem, o_sem,
                   w_in_ref, w_s_ref, b_s_ref, w_out_ref, stage_in, stage_out, stage_sem):
    (n_a, seq_a), (n_b, seq_b) = streams

    def in_copies(hbm, n_tiles, t, to_slot):
        row0 = pl.multiple_of(t * tile, tile)
        before = pl.multiple_of(jnp.maximum(row0 - HALO, 0), HALO)
        after = pl.multiple_of(jnp.minimum(row0 + tile, n_tiles * tile - HALO), HALO)
        return (
            pltpu.make_async_copy(hbm.at[pl.ds(row0, tile), :], x_buf.at[to_slot],
                                  x_sem.at[to_slot]),
            pltpu.make_async_copy(hbm.at[pl.ds(before, HALO), :],
                                  halo_buf.at[to_slot, 0], halo_sem.at[to_slot, 0]),
            pltpu.make_async_copy(hbm.at[pl.ds(after, HALO), :],
                                  halo_buf.at[to_slot, 1], halo_sem.at[to_slot, 1]),
        )

    def out_copy(hbm, t, from_slot):
        return pltpu.make_async_copy(o_buf.at[from_slot],
                                     hbm.at[pl.ds(pl.multiple_of(t * tile, tile), tile), :],
                                     o_sem.at[from_slot])

    def run_tiles(x_hbm, o_hbm, n_tiles, seq_len, first_step, first_slot, t_begin, t_end):
        tiles_per_seq = seq_len // tile
        wait_out = first_step + t_begin >= 2
        assert wait_out or first_step + t_end <= 2

        def step(t, carry):
            g = first_step + t
            oslot = lax.rem(g, 2)
            slot = jnp.where(t == 0, first_slot, oslot)
            if wait_out:
                out_copy(o_hbm, 0, oslot).wait()
            for c in in_copies(x_hbm, n_tiles, jnp.minimum(t + 1, n_tiles - 1), 1 - oslot):
                c.start()
            for c in in_copies(x_hbm, n_tiles, 0, slot):
                c.wait()
            _tile_compute(seq_len, tiles_per_seq, lax.rem(t, tiles_per_seq), tile,
                          x_buf.at[slot], halo_buf.at[slot, 0], halo_buf.at[slot, 1],
                          w_in_ref, ln_g_ref, w_s_ref, b_s_ref, w_out_ref, final_g_ref,
                          o_buf.at[oslot])
            out_copy(o_hbm, t, oslot).start()
            return carry

        if t_begin < t_end:
            lax.fori_loop(t_begin, t_end, step, 0)

    def start_first_tile():
        for c in in_copies(xa_hbm, n_a, 0, 0):
            c.start()

    _prepare_weights(norm_g_ref, pool_w_ref, pool_scale_ref, w_spatial_ref, b_spatial_ref,
                     ln_b_ref, w_in_hbm, w_out_hbm, stage_in, stage_out, stage_sem,
                     w_in_ref, w_s_ref, b_s_ref, w_out_ref, start_first_tile)
    for c in in_copies(xb_hbm, n_b, 0, 2):
        c.start()
    head = min(2, n_a)
    run_tiles(xa_hbm, oa_hbm, n_a, seq_a, 0, 0, 0, head)
    run_tiles(xa_hbm, oa_hbm, n_a, seq_a, 0, 0, head, n_a)
    for c in in_copies(xa_hbm, n_a, 0, n_a % 2):
        c.wait()
    head = max(0, 2 - n_a)
    run_tiles(xb_hbm, ob_hbm, n_b, seq_b, n_a, 2, 0, head)
    run_tiles(xb_hbm, ob_hbm, n_b, seq_b, n_a, 2, head, n_b)
    for c in in_copies(xb_hbm, n_b, 0, (n_a + n_b) % 2):
        c.wait()
    out_copy(oa_hbm, 0, 0).wait()
    out_copy(oa_hbm, 0, 1).wait()


def _layers(xa, xb, norm_g, w_in, pool_w, pool_scale, sgu_ln_g, sgu_ln_b, w_spatial, b_spatial,
            w_out, final_g, tile):
    d = D_MODEL
    assert D_MODEL % PREP_ROWS == 0

    def stream(x):
        bsz, seq_len, dd = x.shape
        assert dd == d and seq_len % tile == 0 and tile % CHUNK == 0 and tile % OUT_ROWS == 0
        n_tiles = bsz * seq_len // tile
        assert n_tiles * tile >= 2 * HALO
        return x.reshape(bsz * seq_len, d), (n_tiles, seq_len)

    xa2, info_a = stream(xa)
    xb2, info_b = stream(xb)
    assert info_a[0] + info_b[0] >= 2

    anywhere = pl.BlockSpec(memory_space=pl.ANY)
    whole = pl.BlockSpec(memory_space=pltpu.VMEM)
    in_specs = [
        anywhere,
        anywhere,
        anywhere,
        anywhere,
        whole,
        whole,
        whole,
        whole,
        whole,
        whole,
        whole,
        whole,
    ]
    oa, ob = pl.pallas_call(
        functools.partial(_layers_kernel, tile, (info_a, info_b)),
        in_specs=in_specs,
        out_specs=[anywhere, anywhere],
        out_shape=[jax.ShapeDtypeStruct(xa2.shape, xa.dtype),
                   jax.ShapeDtypeStruct(xb2.shape, xb.dtype)],
        scratch_shapes=[pltpu.VMEM((3, tile, d), jnp.float32),
                        pltpu.VMEM((3, 2, HALO, d), jnp.float32),
                        pltpu.VMEM((2, tile, d), jnp.float32),
                        pltpu.SemaphoreType.DMA((3,)),
                        pltpu.SemaphoreType.DMA((3, 2)),
                        pltpu.SemaphoreType.DMA((2,)),
                        pltpu.VMEM((D_MODEL, IN_WIDTH), jnp.bfloat16),
                        pltpu.VMEM(w_spatial.shape, jnp.bfloat16),
                        pltpu.VMEM((SGU_HEADS, CHUNK, SGU_HEAD_DIM), jnp.float32),
                        pltpu.VMEM((D_MODEL, D_MODEL + W_OUT_PAD), jnp.bfloat16),
                        pltpu.VMEM((2, PREP_ROWS, IN_WIDTH), jnp.float32),
                        pltpu.VMEM((2, PREP_ROWS, D_MODEL), jnp.float32),
                        pltpu.SemaphoreType.DMA((2, 2))],
        compiler_params=pltpu.CompilerParams(vmem_limit_bytes=VMEM_LIMIT_BYTES),
    )(xa2, xb2, w_in, w_out, norm_g.reshape(1, D_MODEL), pool_w, pool_scale.reshape(1, A_WIDTH),
      w_spatial, b_spatial, sgu_ln_g.reshape(1, B_WIDTH), sgu_ln_b.reshape(1, B_WIDTH),
      final_g.reshape(1, D_MODEL))
    return oa.reshape(xa.shape), ob.reshape(xb.shape)


def kernel(x_prompt, x_sample, norm_g, w_in, pool_w, pool_scale, sgu_ln_g, sgu_ln_b,
           w_spatial, b_spatial, w_out, final_g):
    return _layers(x_prompt, x_sample, norm_g, w_in, pool_w, pool_scale, sgu_ln_g, sgu_ln_b,
                   w_spatial, b_spatial, w_out, final_g, TILE)
```

```python
import functools

import jax
import jax.numpy as jnp
from jax import lax
from jax.experimental import pallas as pl
from jax.experimental.pallas import tpu as pltpu

D_MODEL = 1024
A_WIDTH = 512
B_WIDTH = 512
POOL_WINDOWS = (2, 4, 8, 16)
POOL_CH = 128
CHUNK = 128
SGU_HEADS = 4
SGU_HEAD_DIM = 128
IN_WIDTH = 2 * A_WIDTH + 3 * B_WIDTH
EPS = 1e-6

COL_A, COL_GATE_A, COL_U, COL_V, COL_GATE_B = 0, 512, 1024, 1536, 2048

HALO = max(POOL_WINDOWS) // 2
SUBLANES = 8
TILE = 1024
W_OUT_PAD = 128
OUT_COLS = 512
OUT_ROWS = 256
PREP_ROWS = 512
VMEM_LIMIT_BYTES = 56 * 1024 * 1024


def _normalize(x):
    ms = jnp.mean(x * x, axis=-1, keepdims=True)
    return x * lax.rsqrt(ms + EPS)


def _silu_of_half(hx):
    return hx + hx * jnp.tanh(hx)


def _prepare_weights(norm_g_ref, pool_w_ref, pool_scale_ref, w_spatial_ref, b_spatial_ref, ln_b_ref,
                     w_in_hbm, w_out_hbm, stage_in, stage_out, stage_sem,
                     w_in_o, w_s_o, b_s_o, w_out_o, start_first_tile):
    bf16 = jnp.bfloat16
    n_chunks = D_MODEL // PREP_ROWS

    def as_column(row):
        n = row.shape[-1]
        on_diag = (lax.broadcasted_iota(jnp.int32, (n, n), 0)
                   == lax.broadcasted_iota(jnp.int32, (n, n), 1))
        return jnp.sum(jnp.where(on_diag, row, 0.0), axis=-1, keepdims=True)

    def chunk_copies(k, slot):
        rows = pl.ds(k * PREP_ROWS, PREP_ROWS)
        return (pltpu.make_async_copy(w_in_hbm.at[rows, :], stage_in.at[slot], stage_sem.at[slot, 0]),
                pltpu.make_async_copy(w_out_hbm.at[rows, :], stage_out.at[slot], stage_sem.at[slot, 1]))

    for c in chunk_copies(0, 0):
        c.start()
    for k in range(n_chunks):
        slot = k % 2
        if k + 1 < n_chunks:
            for c in chunk_copies(k + 1, 1 - slot):
                c.start()
        if k == 0:
            start_first_tile()
        for c in chunk_copies(k, slot):
            c.wait()
        rows = slice(k * PREP_ROWS, (k + 1) * PREP_ROWS)
        w_in_blk, w_out_blk = stage_in.at[slot], stage_out.at[slot]
        g = as_column(norm_g_ref[:, rows])
        for grp in range(len(POOL_WINDOWS)):
            cols = slice(grp * POOL_CH, (grp + 1) * POOL_CH)
            folded = jnp.dot(w_in_blk[:, cols] * g, pool_w_ref[grp],
                             preferred_element_type=jnp.float32,
                             precision=lax.Precision.HIGHEST)
            w_in_o[rows, cols] = (folded * pool_scale_ref[:, cols]).astype(bf16)
        for col0, width, col_scale in ((COL_GATE_A, A_WIDTH, 0.5), (COL_U, B_WIDTH, 1.0),
                                       (COL_V, B_WIDTH, 1.0), (COL_GATE_B, B_WIDTH, 0.5)):
            cols = slice(col0, col0 + width)
            w_in_o[rows, cols] = (w_in_blk[:, cols] * (col_scale * g)).astype(bf16)
        w_out_o[rows, :D_MODEL] = w_out_blk[...].astype(bf16)
        w_out_o[rows, D_MODEL:] = jnp.zeros((PREP_ROWS, W_OUT_PAD), bf16)
    w_s_o[...] = w_spatial_ref[...].astype(bf16)
    for hd in range(SGU_HEADS):
        hcols = slice(hd * SGU_HEAD_DIM, (hd + 1) * SGU_HEAD_DIM)
        w_rowsum = jnp.sum(w_spatial_ref[hd], axis=-1, keepdims=True)
        b_s_o[hd] = w_rowsum * ln_b_ref[:, hcols] + as_column(b_spatial_ref[hd:hd + 1, :])


def _tile_compute(seq_len, tiles_per_seq, j, tile,
                  x_ref, xprev_ref, xnext_ref, w_in_ref, ln_g_ref, w_s_ref, b_s_ref,
                  w_out_ref, final_g_ref, o_ref):
    f32, bf16 = jnp.float32, jnp.bfloat16
    n_chunks = tile // CHUNK
    pos0 = j * tile

    def proj(hh, col0, width):
        return jnp.dot(hh, w_in_ref[:, col0:col0 + width], preferred_element_type=f32)

    h = _normalize(x_ref[...]).astype(bf16)

    x_halo = jnp.concatenate([xprev_ref[...], xnext_ref[...]], axis=0)
    a_halo = proj(_normalize(x_halo).astype(bf16), COL_A, A_WIDTH)
    a_ext = jnp.concatenate([jnp.where(j > 0, a_halo[0:HALO], 0.0),
                             proj(h, COL_A, A_WIDTH),
                             jnp.where(j < tiles_per_seq - 1, a_halo[HALO:], 0.0)], axis=0)

    edge_rows = lax.broadcasted_iota(jnp.int32, (HALO, 1), 0)
    n_ext = tile + 2 * HALO
    sublane = lax.broadcasted_iota(jnp.int32, (1, SUBLANES, POOL_CH), 1)

    def shifted(x3, k):
        whole, part = divmod(k, SUBLANES)
        if whole:
            x3 = jnp.concatenate([x3[whole:], x3[:whole]], axis=0)
        if part:
            rot = pltpu.roll(x3, SUBLANES - part, 1)
            nxt = jnp.concatenate([rot[1:], rot[:1]], axis=0)
            x3 = jnp.where(sublane < SUBLANES - part, rot, nxt)
        return x3

    diffs = []
    for g, w in enumerate(POOL_WINDOWS):
        r = w // 2
        ext = a_ext[:, g * POOL_CH:(g + 1) * POOL_CH]
        run, span = ext.reshape(n_ext // SUBLANES, SUBLANES, POOL_CH), 1
        while span < r:
            run = run + shifted(run, span)
            span *= 2
        win = (run + shifted(run, n_ext - r)).reshape(n_ext, POOL_CH)

        def edge_diff(row0):
            t_abs = pos0 + row0 + edge_rows
            cnt = (jnp.minimum(t_abs + r, seq_len) - jnp.maximum(t_abs - r, 0)).astype(f32)
            rows = slice(HALO + row0, 2 * HALO + row0)
            return win[rows] / cnt - ext[rows]

        inner = slice(2 * HALO, tile)
        diffs.append(jnp.concatenate([edge_diff(0), win[inner] * (1.0 / w) - ext[inner],
                                      edge_diff(tile - HALO)], axis=0))

    half_gate_a = proj(h, COL_GATE_A, A_WIDTH)
    v = proj(h, COL_V, B_WIDTH)
    silu_a = _silu_of_half(half_gate_a)
    mix_parts = [(diffs[g] * silu_a[:, g * POOL_CH:(g + 1) * POOL_CH]).astype(bf16)
                 for g in range(len(POOL_WINDOWS))]

    mu = jnp.mean(v, axis=-1, keepdims=True)
    vc = v - mu
    var = jnp.mean(vc * vc, axis=-1, keepdims=True)
    vn = (vc * lax.rsqrt(var + EPS)).astype(bf16)
    half_gate_b = proj(h, COL_GATE_B, B_WIDTH)
    mixed_heads = []
    for hd in range(SGU_HEADS):
        hcols = slice(hd * SGU_HEAD_DIM, (hd + 1) * SGU_HEAD_DIM)
        vn_wide = jnp.concatenate(
            [vn[c * CHUNK:(c + 1) * CHUNK, hcols] for c in range(n_chunks)], axis=-1)
        mw = jnp.dot(w_s_ref[hd], vn_wide, preferred_element_type=f32)
        gain, shift = ln_g_ref[:, hcols], b_s_ref[hd]
        mixed_heads.append(jnp.concatenate(
            [mw[:, c * SGU_HEAD_DIM:(c + 1) * SGU_HEAD_DIM] * gain + shift
             for c in range(n_chunks)], axis=0))
    mixed = jnp.concatenate(mixed_heads, axis=-1)

    u = proj(h, COL_U, B_WIDTH)
    gated = mixed * _silu_of_half(half_gate_b)
    mix_all = jnp.concatenate(mix_parts + [(u * gated).astype(bf16)], axis=-1)

    final_g = final_g_ref[...]
    for rb in range(tile // OUT_ROWS):
        rows = slice(rb * OUT_ROWS, (rb + 1) * OUT_ROWS)
        mix = mix_all[rows, :]
        delta = jnp.concatenate(
            [jnp.dot(mix, w_out_ref[:, c0:c0 + OUT_COLS], preferred_element_type=f32)
             for c0 in range(0, D_MODEL, OUT_COLS)], axis=-1)
        o_ref[rows, :] = _normalize(x_ref[rows, :] + delta) * final_g


def _layers_kernel(tile, streams,
                   xa_hbm, xb_hbm, w_in_hbm, w_out_hbm, norm_g_ref, pool_w_ref, pool_scale_ref,
                   w_spatial_ref, b_spatial_ref, ln_g_ref, ln_b_ref, final_g_ref,
                   oa_hbm, ob_hbm,
                   x_buf, halo_buf, o_buf, x_sem, halo_sem, o_sem,
                   w_in_ref, w_s_ref, b_s_ref, w_out_ref, stage_in, stage_out, stage_sem):
    (n_a, seq_a), (n_b, seq_b) = streams

    def in_copies(hbm, n_tiles, t, to_slot):
        row0 = pl.multiple_of(t * tile, tile)
        before = pl.multiple_of(jnp.maximum(row0 - HALO, 0), HALO)
        after = pl.multiple_of(jnp.minimum(row0 + tile, n_tiles * tile - HALO), HALO)
        return (
            pltpu.make_async_copy(hbm.at[pl.ds(row0, tile), :], x_buf.at[to_slot],
                                  x_sem.at[to_slot]),
            pltpu.make_async_copy(hbm.at[pl.ds(before, HALO), :],
                                  halo_buf.at[to_slot, 0], halo_sem.at[to_slot, 0]),
            pltpu.make_async_copy(hbm.at[pl.ds(after, HALO), :],
                                  halo_buf.at[to_slot, 1], halo_sem.at[to_slot, 1]),
        )

    def out_copy(hbm, t, from_slot):
        return pltpu.make_async_copy(o_buf.at[from_slot],
                                     hbm.at[pl.ds(pl.multiple_of(t * tile, tile), tile), :],
                                     o_sem.at[from_slot])

    def run_tiles(x_hbm, o_hbm, n_tiles, seq_len, first_step, first_slot, t_begin, t_end):
        tiles_per_seq = seq_len // tile
        wait_out = first_step + t_begin >= 2
        assert wait_out or first_step + t_end <= 2

        def step(t, carry):
            g = first_step + t
            oslot = lax.rem(g, 2)
            slot = jnp.where(t == 0, first_slot, oslot)
            if wait_out:
                out_copy(o_hbm, 0, oslot).wait()
            for c in in_copies(x_hbm, n_tiles, jnp.minimum(t + 1, n_tiles - 1), 1 - oslot):
                c.start()
            for c in in_copies(x_hbm, n_tiles, 0, slot):
                c.wait()
            _tile_compute(seq_len, tiles_per_seq, lax.rem(t, tiles_per_seq), tile,
                          x_buf.at[slot], halo_buf.at[slot, 0], halo_buf.at[slot, 1],
                          w_in_ref, ln_g_ref, w_s_ref, b_s_ref, w_out_ref, final_g_ref,
                          o_buf.at[oslot])
            out_copy(o_hbm, t, oslot).start()
            return carry

        if t_begin < t_end:
            lax.fori_loop(t_begin, t_end, step, 0)

    def start_first_tile():
        for c in in_copies(xa_hbm, n_a, 0, 0):
            c.start()

    _prepare_weights(norm_g_ref, pool_w_ref, pool_scale_ref, w_spatial_ref, b_spatial_ref,
                     ln_b_ref, w_in_hbm, w_out_hbm, stage_in, stage_out, stage_sem,
                     w_in_ref, w_s_ref, b_s_ref, w_out_ref, start_first_tile)
    for c in in_copies(xb_hbm, n_b, 0, 2):
        c.start()
    head = min(2, n_a)
    run_tiles(xa_hbm, oa_hbm, n_a, seq_a, 0, 0, 0, head)
    run_tiles(xa_hbm, oa_hbm, n_a, seq_a, 0, 0, head, n_a)
    for c in in_copies(xa_hbm, n_a, 0, n_a % 2):
        c.wait()
    head = max(0, 2 - n_a)
    run_tiles(xb_hbm, ob_hbm, n_b, seq_b, n_a, 2, 0, head)
    run_tiles(xb_hbm, ob_hbm, n_b, seq_b, n_a, 2, head, n_b)
    for c in in_copies(xb_hbm, n_b, 0, (n_a + n_b) % 2):
        c.wait()
    out_copy(oa_hbm, 0, 0).wait()
    out_copy(oa_hbm, 0, 1).wait()


def _layers(xa, xb, norm_g, w_in, pool_w, pool_scale, sgu_ln_g, sgu_ln_b, w_spatial, b_spatial,
            w_out, final_g, tile):
    d = D_MODEL
    assert D_MODEL % PREP_ROWS == 0

    def stream(x):
        bsz, seq_len, dd = x.shape
        assert dd == d and seq_len % tile == 0 and tile % CHUNK == 0 and tile % OUT_ROWS == 0
        n_tiles = bsz * seq_len // tile
        assert n_tiles * tile >= 2 * HALO
        return x.reshape(bsz * seq_len, d), (n_tiles, seq_len)

    xa2, info_a = stream(xa)
    xb2, info_b = stream(xb)
    assert info_a[0] + info_b[0] >= 2

    anywhere = pl.BlockSpec(memory_space=pl.ANY)
    whole = pl.BlockSpec(memory_space=pltpu.VMEM)
    in_specs = [
        anywhere,
        anywhere,
        anywhere,
        anywhere,
        whole,
        whole,
        whole,
        whole,
        whole,
        whole,
        whole,
        whole,
    ]
    oa, ob = pl.pallas_call(
        functools.partial(_layers_kernel, tile, (info_a, info_b)),
        in_specs=in_specs,
        out_specs=[anywhere, anywhere],
        out_shape=[jax.ShapeDtypeStruct(xa2.shape, xa.dtype),
                   jax.ShapeDtypeStruct(xb2.shape, xb.dtype)],
        scratch_shapes=[pltpu.VMEM((3, tile, d), jnp.float32),
                        pltpu.VMEM((3, 2, HALO, d), jnp.float32),
                        pltpu.VMEM((2, tile, d), jnp.float32),
                        pltpu.SemaphoreType.DMA((3,)),
                        pltpu.SemaphoreType.DMA((3, 2)),
                        pltpu.SemaphoreType.DMA((2,)),
                        pltpu.VMEM((D_MODEL, IN_WIDTH), jnp.bfloat16),
                        pltpu.VMEM(w_spatial.shape, jnp.bfloat16),
                        pltpu.VMEM((SGU_HEADS, CHUNK, SGU_HEAD_DIM), jnp.float32),
                        pltpu.VMEM((D_MODEL, D_MODEL + W_OUT_PAD), jnp.bfloat16),
                        pltpu.VMEM((2, PREP_ROWS, IN_WIDTH), jnp.float32),
                        pltpu.VMEM((2, PREP_ROWS, D_MODEL), jnp.float32),
                        pltpu.SemaphoreType.DMA((2, 2))],
        compiler_params=pltpu.CompilerParams(vmem_limit_bytes=VMEM_LIMIT_BYTES),
    )(xa2, xb2, w_in, w_out, norm_g.reshape(1, D_MODEL), pool_w, pool_scale.reshape(1, A_WIDTH),
      w_spatial, b_spatial, sgu_ln_g.reshape(1, B_WIDTH), sgu_ln_b.reshape(1, B_WIDTH),
      final_g.reshape(1, D_MODEL))
    return oa.reshape(xa.shape), ob.reshape(xb.shape)


def kernel(x_prompt, x_sample, norm_g, w_in, pool_w, pool_scale, sgu_ln_g, sgu_ln_b,
           w_spatial, b_spatial, w_out, final_g):
    return _layers(x_prompt, x_sample, norm_g, w_in, pool_w, pool_scale, sgu_ln_g, sgu_ln_b,
                   w_spatial, b_spatial, w_out, final_g, TILE)
```

```python
import functools

import jax
import jax.numpy as jnp
from jax import lax
from jax.experimental import pallas as pl
from jax.experimental.pallas import tpu as pltpu

D_MODEL = 1024
A_WIDTH = 512
B_WIDTH = 512
POOL_WINDOWS = (2, 4, 8, 16)
POOL_CH = 128
CHUNK = 128
SGU_HEADS = 4
SGU_HEAD_DIM = 128
IN_WIDTH = 2 * A_WIDTH + 3 * B_WIDTH
EPS = 1e-6

COL_A, COL_GATE_A, COL_U, COL_V, COL_GATE_B = 0, 512, 1024, 1536, 2048

HALO = max(POOL_WINDOWS) // 2
SUBLANES = 8
TILE = 1024
W_OUT_PAD = 128
OUT_COLS = 512
OUT_ROWS = 256
PREP_ROWS = 512
VMEM_LIMIT_BYTES = 56 * 1024 * 1024


def _normalize(x):
    ms = jnp.mean(x * x, axis=-1, keepdims=True)
    return x * lax.rsqrt(ms + EPS)


def _silu_of_half(hx):
    return hx + hx * jnp.tanh(hx)


def _prepare_weights(norm_g_ref, pool_w_ref, pool_scale_ref, w_spatial_ref, b_spatial_ref, ln_b_ref,
                     w_in_hbm, w_out_hbm, stage_in, stage_out, stage_sem,
                     w_in_o, w_s_o, b_s_o, w_out_o, start_first_tile):
    bf16 = jnp.bfloat16
    n_chunks = D_MODEL // PREP_ROWS

    def as_column(row):
        n = row.shape[-1]
        on_diag = (lax.broadcasted_iota(jnp.int32, (n, n), 0)
                   == lax.broadcasted_iota(jnp.int32, (n, n), 1))
        return jnp.sum(jnp.where(on_diag, row, 0.0), axis=-1, keepdims=True)

    def chunk_copies(k, slot):
        rows = pl.ds(k * PREP_ROWS, PREP_ROWS)
        return (pltpu.make_async_copy(w_in_hbm.at[rows, :], stage_in.at[slot], stage_sem.at[slot, 0]),
                pltpu.make_async_copy(w_out_hbm.at[rows, :], stage_out.at[slot], stage_sem.at[slot, 1]))

    for c in chunk_copies(0, 0):
        c.start()
    for k in range(n_chunks):
        slot = k % 2
        if k + 1 < n_chunks:
            for c in chunk_copies(k + 1, 1 - slot):
                c.start()
        if k == 0:
            start_first_tile()
        for c in chunk_copies(k, slot):
            c.wait()
        rows = slice(k * PREP_ROWS, (k + 1) * PREP_ROWS)
        w_in_blk, w_out_blk = stage_in.at[slot], stage_out.at[slot]
        g = as_column(norm_g_ref[:, rows])
        for grp in range(len(POOL_WINDOWS)):
            cols = slice(grp * POOL_CH, (grp + 1) * POOL_CH)
            folded = jnp.dot(w_in_blk[:, cols] * g, pool_w_ref[grp],
                             preferred_element_type=jnp.float32,
                             precision=lax.Precision.HIGHEST)
            w_in_o[rows, cols] = (folded * pool_scale_ref[:, cols]).astype(bf16)
        for col0, width, col_scale in ((COL_GATE_A, A_WIDTH, 0.5), (COL_U, B_WIDTH, 1.0),
                                       (COL_V, B_WIDTH, 1.0), (COL_GATE_B, B_WIDTH, 0.5)):
            cols = slice(col0, col0 + width)
            w_in_o[rows, cols] = (w_in_blk[:, cols] * (col_scale * g)).astype(bf16)
        w_out_o[rows, :D_MODEL] = w_out_blk[...].astype(bf16)
        w_out_o[rows, D_MODEL:] = jnp.zeros((PREP_ROWS, W_OUT_PAD), bf16)
    w_s_o[...] = w_spatial_ref[...].astype(bf16)
    for hd in range(SGU_HEADS):
        hcols = slice(hd * SGU_HEAD_DIM, (hd + 1) * SGU_HEAD_DIM)
        w_rowsum = jnp.sum(w_spatial_ref[hd], axis=-1, keepdims=True)
        b_s_o[hd] = w_rowsum * ln_b_ref[:, hcols] + as_column(b_spatial_ref[hd:hd + 1, :])


def _tile_compute(seq_len, tiles_per_seq, j, tile,
                  x_ref, xprev_ref, xnext_ref, w_in_ref, ln_g_ref, w_s_ref, b_s_ref,
                  w_out_ref, final_g_ref, o_ref):
    f32, bf16 = jnp.float32, jnp.bfloat16
    n_chunks = tile // CHUNK
    pos0 = j * tile

    def proj(hh, col0, width):
        return jnp.dot(hh, w_in_ref[:, col0:col0 + width], preferred_element_type=f32)

    h = _normalize(x_ref[...]).astype(bf16)

    x_halo = jnp.concatenate([xprev_ref[...], xnext_ref[...]], axis=0)
    a_halo = proj(_normalize(x_halo).astype(bf16), COL_A, A_WIDTH)
    a_ext = jnp.concatenate([jnp.where(j > 0, a_halo[0:HALO], 0.0),
                             proj(h, COL_A, A_WIDTH),
                             jnp.where(j < tiles_per_seq - 1, a_halo[HALO:], 0.0)], axis=0)

    edge_rows = lax.broadcasted_iota(jnp.int32, (HALO, 1), 0)
    n_ext = tile + 2 * HALO
    sublane = lax.broadcasted_iota(jnp.int32, (1, SUBLANES, POOL_CH), 1)

    def shifted(x3, k):
        whole, part = divmod(k, SUBLANES)
        if whole:
            x3 = jnp.concatenate([x3[whole:], x3[:whole]], axis=0)
        if part:
            rot = pltpu.roll(x3, SUBLANES - part, 1)
            nxt = jnp.concatenate([rot[1:], rot[:1]], axis=0)
            x3 = jnp.where(sublane < SUBLANES - part, rot, nxt)
        return x3

    diffs = []
    for g, w in enumerate(POOL_WINDOWS):
        r = w // 2
        ext = a_ext[:, g * POOL_CH:(g + 1) * POOL_CH]
        run, span = ext.reshape(n_ext // SUBLANES, SUBLANES, POOL_CH), 1
        while span < r:
            run = run + shifted(run, span)
            span *= 2
        win = (run + shifted(run, n_ext - r)).reshape(n_ext, POOL_CH)

        def edge_diff(row0):
            t_abs = pos0 + row0 + edge_rows
            cnt = (jnp.minimum(t_abs + r, seq_len) - jnp.maximum(t_abs - r, 0)).astype(f32)
            rows = slice(HALO + row0, 2 * HALO + row0)
            return win[rows] / cnt - ext[rows]

        inner = slice(2 * HALO, tile)
        diffs.append(jnp.concatenate([edge_diff(0), win[inner] * (1.0 / w) - ext[inner],
                                      edge_diff(tile - HALO)], axis=0))

    half_gate_a = proj(h, COL_GATE_A, A_WIDTH)
    v = proj(h, COL_V, B_WIDTH)
    silu_a = _silu_of_half(half_gate_a)
    mix_parts = [(diffs[g] * silu_a[:, g * POOL_CH:(g + 1) * POOL_CH]).astype(bf16)
                 for g in range(len(POOL_WINDOWS))]

    mu = jnp.mean(v, axis=-1, keepdims=True)
    vc = v - mu
    var = jnp.mean(vc * vc, axis=-1, keepdims=True)
    vn = (vc * lax.rsqrt(var + EPS)).astype(bf16)
    half_gate_b = proj(h, COL_GATE_B, B_WIDTH)
    mixed_heads = []
    for hd in range(SGU_HEADS):
        hcols = slice(hd * SGU_HEAD_DIM, (hd + 1) * SGU_HEAD_DIM)
        vn_wide = jnp.concatenate(
            [vn[c * CHUNK:(c + 1) * CHUNK, hcols] for c in range(n_chunks)], axis=-1)
        mw = jnp.dot(w_s_ref[hd], vn_wide, preferred_element_type=f32)
        gain, shift = ln_g_ref[:, hcols], b_s_ref[hd]
        mixed_heads.append(jnp.concatenate(
            [mw[:, c * SGU_HEAD_DIM:(c + 1) * SGU_HEAD_DIM] * gain + shift
             for c in range(n_chunks)], axis=0))
    mixed = jnp.concatenate(mixed_heads, axis=-1)

    u = proj(h, COL_U, B_WIDTH)
    gated = mixed * _silu_of_half(half_gate_b)
    mix_all = jnp.concatenate(mix_parts + [(u * gated).astype(bf16)], axis=-1)

    final_g = final_g_ref[...]
    for rb in range(tile // OUT_ROWS):
        rows = slice(rb * OUT_ROWS, (rb + 1) * OUT_ROWS)
        mix = mix_all[rows, :]
        delta = jnp.concatenate(
            [jnp.dot(mix, w_out_ref[:, c0:c0 + OUT_COLS], preferred_element_type=f32)
             for c0 in range(0, D_MODEL, OUT_COLS)], axis=-1)
        o_ref[rows, :] = _normalize(x_ref[rows, :] + delta) * final_g


def _layers_kernel(tile, streams,
                   xa_hbm, xb_hbm, w_in_hbm, w_out_hbm, norm_g_ref, pool_w_ref, pool_scale_ref,
                   w_spatial_ref, b_spatial_ref, ln_g_ref, ln_b_ref, final_g_ref,
                   oa_hbm, ob_hbm,
                   x_buf, halo_buf, o_buf, x_sem, halo_sem, o_sem,
                   w_in_ref, w_s_ref, b_s_ref, w_out_ref, stage_in, stage_out, stage_sem):
    (n_a, seq_a), (n_b, seq_b) = streams

    def in_copies(hbm, n_tiles, t, to_slot):
        row0 = pl.multiple_of(t * tile, tile)
        before = pl.multiple_of(jnp.maximum(row0 - HALO, 0), HALO)
        after = pl.multiple_of(jnp.minimum(row0 + tile, n_tiles * tile - HALO), HALO)
        return (
            pltpu.make_async_copy(hbm.at[pl.ds(row0, tile), :], x_buf.at[to_slot],
                                  x_sem.at[to_slot]),
            pltpu.make_async_copy(hbm.at[pl.ds(before, HALO), :],
                                  halo_buf.at[to_slot, 0], halo_sem.at[to_slot, 0]),
            pltpu.make_async_copy(hbm.at[pl.ds(after, HALO), :],
                                  halo_buf.at[to_slot, 1], halo_sem.at[to_slot, 1]),
        )

    def out_copy(hbm, t, from_slot):
        return pltpu.make_async_copy(o_buf.at[from_slot],
                                     hbm.at[pl.ds(pl.multiple_of(t * tile, tile), tile), :],
                                     o_sem.at[from_slot])

    def run_tiles(x_hbm, o_hbm, n_tiles, seq_len, first_step, first_slot, t_begin, t_end):
        tiles_per_seq = seq_len // tile
        wait_out = first_step + t_begin >= 2
        assert wait_out or first_step + t_end <= 2

        def step(t, carry):
            g = first_step + t
            oslot = lax.rem(g, 2)
            slot = jnp.where(t == 0, first_slot, oslot)
            if wait_out:
                out_copy(o_hbm, 0, oslot).wait()
            for c in in_copies(x_hbm, n_tiles, 0, slot):
                c.wait()
            _tile_compute(seq_len, tiles_per_seq, lax.rem(t, tiles_per_seq), tile,
                          x_buf.at[slot], halo_buf.at[slot, 0], halo_buf.at[slot, 1],
                          w_in_ref, ln_g_ref, w_s_ref, b_s_ref, w_out_ref, final_g_ref,
                          o_buf.at[oslot])
            out_copy(o_hbm, t, oslot).start()
            for c in in_copies(x_hbm, n_tiles, jnp.minimum(t + 2, n_tiles - 1), oslot):
                c.start()
            return carry

        if t_begin < t_end:
            lax.fori_loop(t_begin, t_end, step, 0)

    def start_first_tile():
        for c in in_copies(xa_hbm, n_a, 0, 0):
            c.start()

    _prepare_weights(norm_g_ref, pool_w_ref, pool_scale_ref, w_spatial_ref, b_spatial_ref,
                     ln_b_ref, w_in_hbm, w_out_hbm, stage_in, stage_out, stage_sem,
                     w_in_ref, w_s_ref, b_s_ref, w_out_ref, start_first_tile)
    for c in in_copies(xa_hbm, n_a, min(1, n_a - 1), 1):
        c.start()
    for c in in_copies(xb_hbm, n_b, 0, 2):
        c.start()
    head = min(2, n_a)
    run_tiles(xa_hbm, oa_hbm, n_a, seq_a, 0, 0, 0, head)
    run_tiles(xa_hbm, oa_hbm, n_a, seq_a, 0, 0, head, n_a)
    for s in (0, 1):
        for c in in_copies(xa_hbm, n_a, 0, s):
            c.wait()
    for c in in_copies(xb_hbm, n_b, min(1, n_b - 1), (n_a + 1) % 2):
        c.start()
    head = max(0, 2 - n_a)
    run_tiles(xb_hbm, ob_hbm, n_b, seq_b, n_a, 2, 0, head)
    run_tiles(xb_hbm, ob_hbm, n_b, seq_b, n_a, 2, head, n_b)
    for s in (0, 1):
        for c in in_copies(xb_hbm, n_b, 0, s):
            c.wait()
    out_copy(oa_hbm, 0, 0).wait()
    out_copy(oa_hbm, 0, 1).wait()


def _layers(xa, xb, norm_g, w_in, pool_w, pool_scale, sgu_ln_g, sgu_ln_b, w_spatial, b_spatial,
            w_out, final_g, tile):
    d = D_MODEL
    assert D_MODEL % PREP_ROWS == 0

    def stream(x):
        bsz, seq_len, dd = x.shape
        assert dd == d and seq_len % tile == 0 and tile % CHUNK == 0 and tile % OUT_ROWS == 0
        n_tiles = bsz * seq_len // tile
        assert n_tiles * tile >= 2 * HALO
        return x.reshape(bsz * seq_len, d), (n_tiles, seq_len)

    xa2, info_a = stream(xa)
    xb2, info_b = stream(xb)
    assert info_a[0] + info_b[0] >= 2

    anywhere = pl.BlockSpec(memory_space=pl.ANY)
    whole = pl.BlockSpec(memory_space=pltpu.VMEM)
    in_specs = [
        anywhere,
        anywhere,
        anywhere,
        anywhere,
        whole,
        whole,
        whole,
        whole,
        whole,
        whole,
        whole,
        whole,
    ]
    oa, ob = pl.pallas_call(
        functools.partial(_layers_kernel, tile, (info_a, info_b)),
        in_specs=in_specs,
        out_specs=[anywhere, anywhere],
        out_shape=[jax.ShapeDtypeStruct(xa2.shape, xa.dtype),
                   jax.ShapeDtypeStruct(xb2.shape, xb.dtype)],
        scratch_shapes=[pltpu.VMEM((3, tile, d), jnp.float32),
                        pltpu.VMEM((3, 2, HALO, d), jnp.float32),
                        pltpu.VMEM((2, tile, d), jnp.float32),
                        pltpu.SemaphoreType.DMA((3,)),
                        pltpu.SemaphoreType.DMA((3, 2)),
                        pltpu.SemaphoreType.DMA((2,)),
                        pltpu.VMEM((D_MODEL, IN_WIDTH), jnp.bfloat16),
                        pltpu.VMEM(w_spatial.shape, jnp.bfloat16),
                        pltpu.VMEM((SGU_HEADS, CHUNK, SGU_HEAD_DIM), jnp.float32),
                        pltpu.VMEM((D_MODEL, D_MODEL + W_OUT_PAD), jnp.bfloat16),
                        pltpu.VMEM((2, PREP_ROWS, IN_WIDTH), jnp.float32),
                        pltpu.VMEM((2, PREP_ROWS, D_MODEL), jnp.float32),
                        pltpu.SemaphoreType.DMA((2, 2))],
        compiler_params=pltpu.CompilerParams(vmem_limit_bytes=VMEM_LIMIT_BYTES),
    )(xa2, xb2, w_in, w_out, norm_g.reshape(1, D_MODEL), pool_w, pool_scale.reshape(1, A_WIDTH),
      w_spatial, b_spatial, sgu_ln_g.reshape(1, B_WIDTH), sgu_ln_b.reshape(1, B_WIDTH),
      final_g.reshape(1, D_MODEL))
    return oa.reshape(xa.shape), ob.reshape(xb.shape)


def kernel(x_prompt, x_sample, norm_g, w_in, pool_w, pool_scale, sgu_ln_g, sgu_ln_b,
           w_spatial, b_spatial, w_out, final_g):
    return _layers(x_prompt, x_sample, norm_g, w_in, pool_w, pool_scale, sgu_ln_g, sgu_ln_b,
                   w_spatial, b_spatial, w_out, final_g, TILE)
```
